```python
import math, functools
import jax, jax.numpy as jnp
from jax import lax
import numpy as np

D_MODEL = 1024
BATCH = 4
SEQ = 4096
DEPTH = 2
DEC_BATCH = 128
DEC_SEQ = 8
PAST_LEN = 2048
PAGE_SIZE = 128

HEAD_DIM = 64
N_HEADS_EVEN = D_MODEL // HEAD_DIM
H_MOBA = N_HEADS_EVEN // 2
H_SB = N_HEADS_EVEN - H_MOBA
N_HEADS_ODD = D_MODEL // HEAD_DIM
MOBA_BLOCK = 256
MOBA_TOPK = 3
DSA_TOPK = 256
IDX_HEADS = 8
IDX_DIM = 64
N_EXPERTS = 32
TOP_K = 4
D_FF = D_MODEL
SWIGLU_LIMIT = 7.0
SWIGLU_ALPHA = 1.702
EXPERT_BLOCK = 128
Q_BLOCK = 128
GATHER_Q_BLOCK = 64
LN_EPS = 1e-5
N_EVEN = (DEPTH + 1) // 2
N_ODD = DEPTH // 2
DEEPNORM_ALPHA = (2 * DEPTH) ** 0.25
DEEPNORM_BETA = (8 * DEPTH) ** -0.25
PROJ_EVEN = 3 * N_HEADS_EVEN * HEAD_DIM
PROJ_ODD = 3 * N_HEADS_ODD * HEAD_DIM + IDX_HEADS * IDX_DIM + IDX_DIM + IDX_HEADS

kernel_name = 'hybrid_moba_stickbreak_dsa_moe_step'


def layer_norm(x, g, b):
    xf = x.astype(jnp.float32)
    mu = xf.mean(-1, keepdims=True)
    var = jnp.square(xf - mu).mean(-1, keepdims=True)
    return ((xf - mu) * lax.rsqrt(var + LN_EPS) * g + b).astype(x.dtype)


def alibi_slopes(n_heads):
    return jnp.exp2(-8.0 * jnp.arange(1, n_heads + 1, dtype=jnp.float32) / n_heads)


def sweep_queries(fn, qs, q_pos, qb):
    B, T = qs[0].shape[:2]
    nb = T // qb
    blocks = tuple(jnp.swapaxes(a.reshape(B, nb, qb, *a.shape[2:]), 0, 1) for a in qs)
    out = lax.map(lambda args: fn(*args[0], args[1]), (blocks, q_pos.reshape(nb, qb)))
    out = jnp.swapaxes(out, 0, 1)
    return out.reshape(B, T, *out.shape[3:])


def gather_past(pool, layer, page_table):
    rows = pool[layer, page_table]
    return rows.reshape(rows.shape[0], rows.shape[1] * rows.shape[2], *rows.shape[3:])


def moba_keys(k, v):
    B, L, H, dh = k.shape
    nb = -(-L // MOBA_BLOCK)
    pad = ((0, 0), (0, nb * MOBA_BLOCK - L), (0, 0), (0, 0))
    kb = jnp.pad(k, pad).reshape(B, nb, MOBA_BLOCK, H, dh).transpose(0, 3, 1, 2, 4)
    vb = jnp.pad(v, pad).reshape(B, nb, MOBA_BLOCK, H, dh).transpose(0, 3, 1, 2, 4)
    kmean = kb.astype(jnp.float32).mean(axis=3)
    return kb, vb, kmean


def moba_block(q, q_pos, kb, vb, kmean, slopes):
    B, Q, H, dh = q.shape
    nb = kb.shape[2]
    n_top = min(MOBA_TOPK, nb)
    cur = q_pos // MOBA_BLOCK
    gate = jnp.einsum('bqhd,bhnd->bhqn', q.astype(jnp.float32), kmean)
    fully_past = jnp.arange(nb)[None, :] < cur[:, None]
    gate = jnp.where(fully_past, gate, -jnp.inf)
    top = lax.top_k(gate, n_top)[1]
    own = jnp.broadcast_to(cur[None, None, :, None], (B, H, Q, 1)).astype(top.dtype)
    sel = jnp.concatenate([top, own], axis=-1)
    bi = jnp.arange(B)[:, None, None, None]
    hi = jnp.arange(H)[None, :, None, None]
    ks = kb[bi, hi, sel]
    vs = vb[bi, hi, sel]
    kpos = sel[..., None] * MOBA_BLOCK + jnp.arange(MOBA_BLOCK)
    slot_live = jnp.concatenate([jnp.arange(n_top)[None, :] < cur[:, None],
                                 jnp.ones((Q, 1), bool)], axis=-1)
    qp = q_pos[None, None, :, None, None]
    valid = slot_live[None, None, :, :, None] & (kpos <= qp)
    s = jnp.einsum('bqhd,bhqnkd->bhqnk', q, ks).astype(jnp.float32) * HEAD_DIM ** -0.5
    s = s - slopes[None, :, None, None, None] * (qp - kpos).astype(jnp.float32)
    s = jnp.where(valid, s, -jnp.inf)
    p = jax.nn.softmax(s.reshape(B, H, Q, -1), axis=-1).reshape(s.shape)
    return jnp.einsum('bhqnk,bhqnkd->bqhd', p.astype(vs.dtype), vs)


def stick_breaking_block(q, q_pos, k, v):
    L = k.shape[1]
    z = jnp.einsum('bqhd,bshd->bhqs', q, k).astype(jnp.float32) * HEAD_DIM ** -0.5
    earlier = jnp.arange(L)[None, :] < q_pos[:, None]
    log_keep = jnp.where(earlier, jax.nn.log_sigmoid(-z), 0.0)
    log_tail = lax.cumsum(log_keep, axis=3, reverse=True) - log_keep
    w = jnp.where(earlier, jnp.exp(jax.nn.log_sigmoid(z) + log_tail), 0.0)
    return jnp.einsum('bhqs,bshd->bqhd', w.astype(v.dtype), v)


def dsa_block(q, qi, wi, q_pos, k, v, kidx, slopes, n_sel):
    B = q.shape[0]
    L = k.shape[1]
    dots = jnp.einsum('bqid,bsd->bqis', qi, kidx).astype(jnp.float32)
    score = jnp.einsum('bqi,bqis->bqs', wi.astype(jnp.float32), jax.nn.relu(dots))
    admissible = jnp.arange(L)[None, :] <= q_pos[:, None]
    score = jnp.where(admissible, score, -jnp.inf)
    sel = lax.top_k(score, n_sel)[1]
    bi = jnp.arange(B)[:, None, None]
    ks = k[bi, sel]
    vs = v[bi, sel]
    dist = q_pos[None, :, None] - sel
    s = jnp.einsum('bqhd,bqnhd->bhqn', q, ks).astype(jnp.float32) * HEAD_DIM ** -0.5
    s = s - slopes[None, :, None, None] * dist[:, None].astype(jnp.float32)
    s = jnp.where((dist >= 0)[:, None], s, -jnp.inf)
    p = jax.nn.softmax(s, axis=-1)
    return jnp.einsum('bhqn,bqnhd->bqhd', p.astype(vs.dtype), vs)


def mixer_even(x, past_kv, qbs, w_in, w_o):
    B, T, _ = x.shape
    qkv = (x @ w_in).reshape(B, T, 3, N_HEADS_EVEN, HEAD_DIM)
    q, new_kv = qkv[:, :, 0], qkv[:, :, 1:]
    kv = new_kv if past_kv is None else jnp.concatenate([past_kv, new_kv], axis=1)
    q_pos = (kv.shape[1] - T) + jnp.arange(T, dtype=jnp.int32)
    k, v = kv[:, :, 0], kv[:, :, 1]
    kb, vb, kmean = moba_keys(k[:, :, :H_MOBA], v[:, :, :H_MOBA])
    slopes = alibi_slopes(H_MOBA)
    o_moba = sweep_queries(lambda qa, p: moba_block(qa, p, kb, vb, kmean, slopes),
                           (q[:, :, :H_MOBA],), q_pos, qbs[0])
    k_sb, v_sb = k[:, :, H_MOBA:], v[:, :, H_MOBA:]
    o_sb = sweep_queries(lambda qb_, p: stick_breaking_block(qb_, p, k_sb, v_sb),
                         (q[:, :, H_MOBA:],), q_pos, qbs[1])
    o = jnp.concatenate([o_moba, o_sb], axis=2).reshape(B, T, N_HEADS_EVEN * HEAD_DIM)
    return o @ w_o, new_kv


def mixer_odd(x, past_kv, past_kidx, qbs, w_in, w_o):
    B, T, _ = x.shape
    proj = x @ w_in
    n_qkv = 3 * N_HEADS_ODD * HEAD_DIM
    n_qi = IDX_HEADS * IDX_DIM
    qkv = proj[..., :n_qkv].reshape(B, T, 3, N_HEADS_ODD, HEAD_DIM)
    qi = proj[..., n_qkv:n_qkv + n_qi].reshape(B, T, IDX_HEADS, IDX_DIM)
    new_kidx = proj[..., n_qkv + n_qi:n_qkv + n_qi + IDX_DIM]
    wi = proj[..., n_qkv + n_qi + IDX_DIM:]
    q, new_kv = qkv[:, :, 0], qkv[:, :, 1:]
    if past_kv is None:
        kv, kidx = new_kv, new_kidx
    else:
        kv = jnp.concatenate([past_kv, new_kv], axis=1)
        kidx = jnp.concatenate([past_kidx, new_kidx], axis=1)
    L = kv.shape[1]
    q_pos = (L - T) + jnp.arange(T, dtype=jnp.int32)
    n_sel = min(DSA_TOPK, L // 4)
    k, v = kv[:, :, 0], kv[:, :, 1]
    slopes = alibi_slopes(N_HEADS_ODD)
    o = sweep_queries(lambda qq, qq_i, ww, p: dsa_block(qq, qq_i, ww, p, k, v, kidx, slopes, n_sel),
                      (q, qi, wi), q_pos, qbs[0])
    return o.reshape(B, T, N_HEADS_ODD * HEAD_DIM) @ w_o, new_kv, new_kidx


def moe_ffn(x, w_router, b_router, w_gate_up, b_gate_up, w_down, b_down):
    shape = x.shape
    xt = x.reshape(-1, D_MODEL)
    T = xt.shape[0]
    logits = (xt @ w_router + b_router).astype(jnp.float32)
    top_val, top_idx = lax.top_k(logits, TOP_K)
    gate = jax.nn.softmax(top_val, axis=-1)
    n_assign = T * TOP_K
    e_flat = top_idx.reshape(n_assign)
    tok_flat = jnp.repeat(jnp.arange(T, dtype=jnp.int32), TOP_K)
    order = jnp.argsort(e_flat, stable=True)
    e_sorted = e_flat[order]
    counts = jnp.bincount(e_flat, length=N_EXPERTS)
    padded = (counts + EXPERT_BLOCK - 1) // EXPERT_BLOCK * EXPERT_BLOCK
    start = jnp.cumsum(counts) - counts
    pad_end = jnp.cumsum(padded)
    pad_start = pad_end - padded
    slot = pad_start[e_sorted] + jnp.arange(n_assign, dtype=jnp.int32) - start[e_sorted]
    n_blocks = -(-n_assign // EXPERT_BLOCK) + N_EXPERTS
    n_slots = n_blocks * EXPERT_BLOCK
    slot_tok = jnp.full((n_slots,), T, jnp.int32).at[slot].set(tok_flat[order])
    slot_gate = jnp.zeros((n_slots,), jnp.float32).at[slot].set(gate.reshape(n_assign)[order])
    block_expert = jnp.minimum(
        jnp.searchsorted(pad_end, jnp.arange(n_blocks) * EXPERT_BLOCK, side='right'), N_EXPERTS - 1)
    x_pad = jnp.concatenate([xt, jnp.zeros((1, D_MODEL), xt.dtype)], axis=0)
    x_blocks = x_pad[slot_tok].reshape(n_blocks, EXPERT_BLOCK, D_MODEL)

    def expert_block(args):
        xb, e = args
        h = xb @ w_gate_up[e] + b_gate_up[e]
        glu = jnp.minimum(h[:, :D_FF], SWIGLU_LIMIT)
        lin = jnp.clip(h[:, D_FF:], -SWIGLU_LIMIT, SWIGLU_LIMIT)
        act = glu * jax.nn.sigmoid(SWIGLU_ALPHA * glu) * (lin + 1.0)
        return act @ w_down[e] + b_down[e]

    y_slots = lax.map(expert_block, (x_blocks, block_expert)).reshape(n_slots, D_MODEL)
    y = jnp.zeros((T + 1, D_MODEL), jnp.float32).at[slot_tok].add(
        y_slots.astype(jnp.float32) * slot_gate[:, None])
    return y[:T].astype(x.dtype).reshape(shape)


def run_group(x, past, qbs, w_in_even, w_o_even, w_in_odd, w_o_odd, ln_g, ln_b,
              w_router, b_router, w_gate_up, b_gate_up, w_down, b_down):
    new_kv_even, new_kv_odd, new_kidx_odd = [], [], []
    for layer in range(DEPTH):
        i = layer // 2
        if layer % 2 == 0:
            past_kv = None if past is None else gather_past(past[0], i, past[3])
            mix, kv = mixer_even(x, past_kv, qbs, w_in_even[i], w_o_even[i])
            new_kv_even.append(kv)
        else:
            past_kv = None if past is None else gather_past(past[1], i, past[3])
            past_kidx = None if past is None else gather_past(past[2], i, past[3])
            mix, kv, kidx = mixer_odd(x, past_kv, past_kidx, qbs, w_in_odd[i], w_o_odd[i])
            new_kv_odd.append(kv)
            new_kidx_odd.append(kidx)
        x = layer_norm(DEEPNORM_ALPHA * x + mix, ln_g[layer, 0], ln_b[layer, 0])
        ffn = moe_ffn(x, w_router[layer], b_router[layer], w_gate_up[layer], b_gate_up[layer],
                      w_down[layer], b_down[layer])
        x = layer_norm(DEEPNORM_ALPHA * x + ffn, ln_g[layer, 1], ln_b[layer, 1])
    return x, jnp.stack(new_kv_even), jnp.stack(new_kv_odd), jnp.stack(new_kidx_odd)


def setup_inputs(seed: int = 0) -> dict:
    key = jax.random.key(seed)
    ks = jax.random.split(key, 20)
    f32 = jnp.float32
    n_pages = PAST_LEN // PAGE_SIZE
    n_used = DEC_BATCH * n_pages
    n_pool = n_used + n_used // 4

    def nrm(k, shape, scale=1.0):
        return jax.random.normal(k, shape, f32) * scale

    page_table = jax.random.permutation(ks[0], n_pool)[:n_used].reshape(DEC_BATCH, n_pages).astype(jnp.int32)
    hd_e = N_HEADS_EVEN * HEAD_DIM
    hd_o = N_HEADS_ODD * HEAD_DIM
    v_even = jnp.ones((PROJ_EVEN,), f32).at[2 * hd_e:3 * hd_e].set(DEEPNORM_BETA)
    v_odd = jnp.ones((PROJ_ODD,), f32).at[2 * hd_o:3 * hd_o].set(DEEPNORM_BETA)
    return {
        'x_prompt': nrm(ks[1], (BATCH, SEQ, D_MODEL)),
        'x_sample': nrm(ks[2], (DEC_BATCH, DEC_SEQ, D_MODEL)),
        'cache_kv_even': nrm(ks[3], (N_EVEN, n_pool, PAGE_SIZE, 2, N_HEADS_EVEN, HEAD_DIM)),
        'cache_kv_odd': nrm(ks[4], (N_ODD, n_pool, PAGE_SIZE, 2, N_HEADS_ODD, HEAD_DIM)),
        'cache_kidx_odd': nrm(ks[5], (N_ODD, n_pool, PAGE_SIZE, IDX_DIM)),
        'page_table': page_table,
        'w_in_even': nrm(ks[6], (N_EVEN, D_MODEL, PROJ_EVEN), D_MODEL ** -0.5) * v_even,
        'w_o_even': nrm(ks[7], (N_EVEN, hd_e, D_MODEL), DEEPNORM_BETA * hd_e ** -0.5),
        'w_in_odd': nrm(ks[8], (N_ODD, D_MODEL, PROJ_ODD), D_MODEL ** -0.5) * v_odd,
        'w_o_odd': nrm(ks[9], (N_ODD, hd_o, D_MODEL), DEEPNORM_BETA * hd_o ** -0.5),
        'ln_g': 1.0 + nrm(ks[10], (DEPTH, 2, D_MODEL), 0.02),
        'ln_b': nrm(ks[11], (DEPTH, 2, D_MODEL), 0.02),
        'w_router': nrm(ks[12], (DEPTH, D_MODEL, N_EXPERTS), D_MODEL ** -0.5),
        'b_router': nrm(ks[13], (DEPTH, N_EXPERTS), 0.01),
        'w_gate_up': nrm(ks[14], (DEPTH, N_EXPERTS, D_MODEL, 2 * D_FF), D_MODEL ** -0.5),
        'b_gate_up': nrm(ks[15], (DEPTH, N_EXPERTS, 2 * D_FF), 0.02),
        'w_down': nrm(ks[16], (DEPTH, N_EXPERTS, D_FF, D_MODEL), DEEPNORM_BETA * D_FF ** -0.5),
        'b_down': nrm(ks[17], (DEPTH, N_EXPERTS, D_MODEL), 0.02),
    }


def reference(x_prompt, x_sample, cache_kv_even, cache_kv_odd, cache_kidx_odd, page_table,
              w_in_even, w_o_even, w_in_odd, w_o_odd, ln_g, ln_b,
              w_router, b_router, w_gate_up, b_gate_up, w_down, b_down):
    y_prompt, kve_p, kvo_p, kio_p = run_group(
        x_prompt, None, (GATHER_Q_BLOCK, Q_BLOCK),
        w_in_even, w_o_even, w_in_odd, w_o_odd, ln_g, ln_b,
        w_router, b_router, w_gate_up, b_gate_up, w_down, b_down)
    y_sample, kve_s, kvo_s, kio_s = run_group(
        x_sample, (cache_kv_even, cache_kv_odd, cache_kidx_odd, page_table), (1, 1),
        w_in_even, w_o_even, w_in_odd, w_o_odd, ln_g, ln_b,
        w_router, b_router, w_gate_up, b_gate_up, w_down, b_down)
    return (y_prompt, y_sample, kve_p, kve_s, kvo_p, kvo_s, kio_p, kio_s)
```

```python
import functools

import jax
import jax.numpy as jnp
from jax import lax
from jax.experimental import pallas as pl
from jax.experimental.pallas import tpu as pltpu

F32 = jnp.float32
BF16 = jnp.bfloat16
I32 = jnp.int32

D_MODEL = 1024
HEAD_DIM = 64
N_HEADS = 16
H_MOBA = 8
MOBA_BLOCK = 256
MOBA_TOPK = 3
DSA_TOPK = 256
IDX_HEADS = 8
IDX_DIM = 64
N_EXPERTS = 32
TOP_K = 4
D_FF = 1024
SWIGLU_LIMIT = 7.0
SWIGLU_ALPHA = 1.702
LN_EPS = 1e-5
DEPTH = 2
DEEPNORM_ALPHA = (2 * DEPTH) ** 0.25
PAGE_SIZE = 128
PAST_LEN = 2048
N_PAGES = PAST_LEN // PAGE_SIZE
DEC_SEQ = 8

LANES = 128
TQ = 256
MOE_BLOCK = 256
ROW_TILE = 512
NEG_MASKED = -2e30
NEG_INIT = -1e30
INT_MIN = -2 ** 31
VMEM_LIMIT = 56 * 1024 * 1024
SCALE = HEAD_DIM ** -0.5


def _cparams(n_grid):
    return pltpu.CompilerParams(dimension_semantics=("arbitrary",) * n_grid,
                                vmem_limit_bytes=VMEM_LIMIT)


def _dot(a, b):
    return jnp.dot(a, b, preferred_element_type=F32)


def _dot_nt(a, b):
    return lax.dot_general(a, b, (((1,), (1,)), ((), ())), preferred_element_type=F32)


def _split_dot(x, tri):
    hi = x.astype(BF16)
    lo = (x - hi.astype(F32)).astype(BF16)
    return _dot(hi, tri) + _dot(lo, tri)


def _log_sigmoid_neg(z):
    return -(jnp.maximum(z, 0.0) + jnp.log(1.0 + jnp.exp(-jnp.abs(z))))


def _proj_kernel(x_ref, *refs, n_out):
    xb = x_ref[...].astype(BF16)
    for w_ref, o_ref in zip(refs[:n_out], refs[n_out:]):
        o_ref[...] = _dot(xb, w_ref[...])


def _proj(x, row0, n_rows, ws, tm=ROW_TILE):
    assert row0 % tm == 0 and n_rows % tm == 0
    blk0 = row0 // tm
    n = len(ws)
    return pl.pallas_call(
        functools.partial(_proj_kernel, n_out=n),
        out_shape=[jax.ShapeDtypeStruct((n_rows, w.shape[1]), F32) for w in ws],
        grid=(n_rows // tm,),
        in_specs=[pl.BlockSpec((tm, D_MODEL), lambda i: (i + blk0, 0))]
        + [pl.BlockSpec(w.shape, lambda i: (0, 0)) for w in ws],
        out_specs=[pl.BlockSpec((tm, w.shape[1]), lambda i: (i, 0)) for w in ws],
        compiler_params=_cparams(1),
        name="proj",
    )(x, *ws)


def _head_queries(q, scale):
    lane = lax.broadcasted_iota(I32, q.shape, 1)
    qs = q * scale
    return [jnp.where((lane >= hh * HEAD_DIM) & (lane < (hh + 1) * HEAD_DIM), qs, 0.0).astype(BF16)
            for hh in range(2)]


def _softmax_tile(s, kv_b, state):
    m, l, acc = state
    m_new = jnp.maximum(m, jnp.max(s, axis=1, keepdims=True))
    p = jnp.exp(s - m_new)
    alpha = jnp.exp(m - m_new)
    l = alpha * l + jnp.sum(p, axis=1, keepdims=True)
    acc = alpha * acc + _dot(p.astype(BF16), kv_b)
    return m_new, l, acc


def _init_state():
    return (jnp.full((TQ, 1), NEG_INIT, F32), jnp.zeros((TQ, 1), F32), jnp.zeros((TQ, LANES), F32))


def _finish_pair(states):
    lane = lax.broadcasted_iota(I32, (TQ, LANES), 1)
    o0 = states[0][2] / states[0][1]
    o1 = states[1][2] / states[1][1]
    return jnp.where(lane < HEAD_DIM, o0, o1)


def _moba_kernel(slopes_ref, q_ref, k_ref, v_ref, o_ref, kmean_sc, *, nt):
    hp = pl.program_id(1)
    i = pl.program_id(2)

    @pl.when(i == 0)
    def _():
        for n in range(nt):
            kmean_sc[n:n + 1, :] = jnp.sum(k_ref[n * TQ:(n + 1) * TQ, :], axis=0, keepdims=True) * (1.0 / MOBA_BLOCK)

    q = q_ref[...]
    q_gate = _head_queries(q, 1.0)
    q_att = _head_queries(q, SCALE)
    km = kmean_sc[...].astype(BF16)
    blk = lax.broadcasted_iota(I32, (TQ, nt), 1)
    past = blk < i
    row = lax.broadcasted_iota(I32, (TQ, TQ), 0)
    col = lax.broadcasted_iota(I32, (TQ, TQ), 1)
    rel = (row - col).astype(F32)

    sels, slopes = [], []
    for hh in range(2):
        gate = jnp.where(past, _dot_nt(q_gate[hh], km), -jnp.inf)
        rank = jnp.zeros((TQ, nt), I32)
        for n2 in range(nt):
            gn = gate[:, n2:n2 + 1]
            rank = rank + ((gn > gate) | ((gn == gate) & (n2 < blk))).astype(I32)
        sels.append((past & (rank < MOBA_TOPK)).astype(F32))
        slopes.append(slopes_ref[hp * 2 + hh])

    start = pl.multiple_of(i * TQ, TQ)
    kb = k_ref[pl.ds(start, TQ), :].astype(BF16)
    vb = v_ref[pl.ds(start, TQ), :].astype(BF16)
    states = []
    for hh in range(2):
        s = _dot_nt(q_att[hh], kb) - slopes[hh] * rel
        s = jnp.where(col <= row, s, NEG_MASKED)
        states.append(_softmax_tile(s, vb, _init_state()))

    def body(j, carry):
        st = [carry[0:3], carry[3:6]]
        off = pl.multiple_of(j * TQ, TQ)
        kb = k_ref[pl.ds(off, TQ), :].astype(BF16)
        vb = v_ref[pl.ds(off, TQ), :].astype(BF16)
        base = ((i - j) * TQ).astype(F32)
        out = []
        for hh in range(2):
            sel_j = jnp.sum(jnp.where(blk == j, sels[hh], 0.0), axis=1, keepdims=True) > 0.0
            s = _dot_nt(q_att[hh], kb) - slopes[hh] * (rel + base)
            s = jnp.where(sel_j, s, NEG_MASKED)
            out.extend(_softmax_tile(s, vb, st[hh]))
        return tuple(out)

    carry = lax.fori_loop(0, i, body, tuple(states[0]) + tuple(states[1]))
    o_ref[...] = _finish_pair([carry[0:3], carry[3:6]])


def _moba_prompt(q, kv, slopes, batch, seq):
    nt = seq // TQ
    return pl.pallas_call(
        functools.partial(_moba_kernel, nt=nt),
        out_shape=jax.ShapeDtypeStruct((batch * seq, H_MOBA * HEAD_DIM), F32),
        grid=(batch, H_MOBA // 2, nt),
        in_specs=[
            pl.BlockSpec(memory_space=pltpu.SMEM),
            pl.BlockSpec((TQ, LANES), lambda b, hp, i: (b * nt + i, hp)),
            pl.BlockSpec((seq, LANES), lambda b, hp, i: (b, hp)),
            pl.BlockSpec((seq, LANES), lambda b, hp, i: (b, N_HEADS // 2 + hp)),
        ],
        out_specs=pl.BlockSpec((TQ, LANES), lambda b, hp, i: (b * nt + i, hp)),
        scratch_shapes=[pltpu.VMEM((nt, LANES), F32)],
        compiler_params=_cparams(3),
        name="moba_prompt",
    )(slopes, q, kv, kv)


def _sb_tile(qh, kb, vb, tri, carry, acc, strict_mask):
    z = _dot_nt(qh, kb)
    lk = _log_sigmoid_neg(z)
    if strict_mask is not None:
        lk = jnp.where(strict_mask, lk, 0.0)
    incl = _split_dot(lk, tri)
    w = jnp.exp(z + incl + carry)
    if strict_mask is not None:
        w = jnp.where(strict_mask, w, 0.0)
    return carry + incl[:, 0:1], acc + _dot(w.astype(BF16), vb)


def _sb_kernel(q_ref, k_ref, v_ref, o_ref):
    i = pl.program_id(2)
    q_att = _head_queries(q_ref[...], SCALE)
    row = lax.broadcasted_iota(I32, (TQ, TQ), 0)
    col = lax.broadcasted_iota(I32, (TQ, TQ), 1)
    tri = (row >= col).astype(BF16)
    strict = col < row

    start = pl.multiple_of(i * TQ, TQ)
    kb = k_ref[pl.ds(start, TQ), :].astype(BF16)
    vb = v_ref[pl.ds(start, TQ), :].astype(BF16)
    init = []
    for hh in range(2):
        init.extend(_sb_tile(q_att[hh], kb, vb, tri, jnp.zeros((TQ, 1), F32),
                             jnp.zeros((TQ, LANES), F32), strict))

    def body(t, carry):
        j = i - 1 - t
        off = pl.multiple_of(j * TQ, TQ)
        kb = k_ref[pl.ds(off, TQ), :].astype(BF16)
        vb = v_ref[pl.ds(off, TQ), :].astype(BF16)
        out = []
        for hh in range(2):
            out.extend(_sb_tile(q_att[hh], kb, vb, tri, carry[2 * hh], carry[2 * hh + 1], None))
        return tuple(out)

    carry = lax.fori_loop(0, i, body, tuple(init))
    lane = lax.broadcasted_iota(I32, (TQ, LANES), 1)
    o_ref[...] = jnp.where(lane < HEAD_DIM, carry[1], carry[3])


def _sb_prompt(q, kv, batch, seq):
    nt = seq // TQ
    n_sb = N_HEADS - H_MOBA
    hp0 = H_MOBA // 2
    return pl.pallas_call(
        _sb_kernel,
        out_shape=jax.ShapeDtypeStruct((batch * seq, n_sb * HEAD_DIM), F32),
        grid=(batch, n_sb // 2, nt),
        in_specs=[
            pl.BlockSpec((TQ, LANES), lambda b, hp, i: (b * nt + i, hp0 + hp)),
            pl.BlockSpec((seq, LANES), lambda b, hp, i: (b, hp0 + hp)),
            pl.BlockSpec((seq, LANES), lambda b, hp, i: (b, N_HEADS // 2 + hp0 + hp)),
        ],
        out_specs=pl.BlockSpec((TQ, LANES), lambda b, hp, i: (b * nt + i, hp)),
        compiler_params=_cparams(3),
        name="sb_prompt",
    )(q, kv, kv)


def _ordered_key(score):
    bits = lax.bitcast_convert_type(score + 0.0, I32)
    return jnp.where(bits < 0, bits ^ 0x7FFFFFFF, bits)


def _kth_largest_key(count_ge, rows, k):
    t = jnp.where(count_ge(jnp.zeros((rows, 1), I32)) >= k, 0, INT_MIN).astype(I32)

    def body(b, t):
        cand = t | (1 << (30 - b))
        return jnp.where(count_ge(cand) >= k, cand, t)

    return lax.fori_loop(0, 31, body, t)


def _dsa_select_kernel(qi_ref, wi_ref, kidx_ref, mask_ref, key_sc, *, nt):
    i = pl.program_id(1)
    qi = qi_ref[...]
    wi = wi_ref[...]
    row = lax.broadcasted_iota(I32, (TQ, TQ), 0)
    col = lax.broadcasted_iota(I32, (TQ, TQ), 1)
    qh = []
    for p in range(IDX_HEADS // 2):
        qh.extend(_head_queries(qi[:, p * LANES:(p + 1) * LANES], 1.0))
    wcol = [wi[:, h:h + 1] for h in range(IDX_HEADS)]

    def score_tile(j):
        off = pl.multiple_of(j * TQ, TQ)
        kb = kidx_ref[pl.ds(off, TQ), :].astype(BF16)
        score = jnp.zeros((TQ, TQ), F32)
        for h in range(IDX_HEADS):
            score = score + wcol[h] * jnp.maximum(_dot_nt(qh[h], kb), 0.0)
        return _ordered_key(score)

    def fill(j, c):
        off = pl.multiple_of(j * TQ, TQ)
        key_sc[:, pl.ds(off, TQ)] = score_tile(j)
        return c

    lax.fori_loop(0, i, fill, 0)
    start = pl.multiple_of(i * TQ, TQ)
    key_sc[:, pl.ds(start, TQ)] = jnp.where(col <= row, score_tile(i), INT_MIN)

    def count(pred):
        def body(j, acc):
            off = pl.multiple_of(j * TQ, TQ)
            hit = jnp.where(pred(key_sc[:, pl.ds(off, TQ)]), 1.0, 0.0)
            return acc + hit[:, :LANES] + hit[:, LANES:]
        acc = lax.fori_loop(0, i + 1, body, jnp.zeros((TQ, LANES), F32))
        return jnp.sum(acc, axis=1, keepdims=True)

    thr = _kth_largest_key(lambda cand: count(lambda k: k >= cand), TQ, DSA_TOPK)
    need = DSA_TOPK - count(lambda k: k > thr)
    tri = (row <= col).astype(BF16)

    mask_ref[...] = jnp.zeros(mask_ref.shape, mask_ref.dtype)

    def emit(j, run):
        off = pl.multiple_of(j * TQ, TQ)
        key = key_sc[:, pl.ds(off, TQ)]
        eq = key == thr
        eqf = jnp.where(eq, 1.0, 0.0)
        prefix = _dot(eqf.astype(BF16), tri) + run
        keep = ((key > thr) | (eq & (prefix <= need))) & (key != INT_MIN)
        mask_ref[:, pl.ds(off, TQ)] = jnp.where(keep, 1.0, 0.0).astype(mask_ref.dtype)
        return run + jnp.sum(eqf, axis=1, keepdims=True)

    lax.fori_loop(0, i + 1, emit, jnp.zeros((TQ, 1), F32))


def _dsa_select_prompt(qi, wi, kidx2, batch, seq):
    nt = seq // TQ
    return pl.pallas_call(
        functools.partial(_dsa_select_kernel, nt=nt),
        out_shape=jax.ShapeDtypeStruct((batch * seq, seq), BF16),
        grid=(batch, nt),
        in_specs=[
            pl.BlockSpec((TQ, IDX_HEADS * IDX_DIM), lambda b, i: (b * nt + i, 0)),
            pl.BlockSpec((TQ, LANES), lambda b, i: (b * nt + i, 0)),
            pl.BlockSpec((seq, LANES), lambda b, i: (b, 0)),
        ],
        out_specs=pl.BlockSpec((TQ, seq), lambda b, i: (b * nt + i, 0)),
        scratch_shapes=[pltpu.VMEM((TQ, seq), I32)],
        compiler_params=_cparams(2),
        name="dsa_select_prompt",
    )(qi, wi, kidx2)


def _dsa_attn_kernel(slopes_ref, q_ref, k_ref, v_ref, mask_ref, o_ref):
    hp = pl.program_id(1)
    i = pl.program_id(2)
    q_att = _head_queries(q_ref[...], SCALE)
    row = lax.broadcasted_iota(I32, (TQ, TQ), 0)
    col = lax.broadcasted_iota(I32, (TQ, TQ), 1)
    rel = (row - col).astype(F32)
    slopes = [slopes_ref[hp * 2 + hh] for hh in range(2)]

    def body(j, carry):
        st = [carry[0:3], carry[3:6]]
        off = pl.multiple_of(j * TQ, TQ)
        kb = k_ref[pl.ds(off, TQ), :].astype(BF16)
        vb = v_ref[pl.ds(off, TQ), :].astype(BF16)
        keep = mask_ref[:, pl.ds(off, TQ)] > 0
        base = ((i - j) * TQ).astype(F32)
        out = []
        for hh in range(2):
            s = _dot_nt(q_att[hh], kb) - slopes[hh] * (rel + base)
            s = jnp.where(keep, s, NEG_MASKED)
            out.extend(_softmax_tile(s, vb, st[hh]))
        return tuple(out)

    carry = lax.fori_loop(0, i + 1, body, _init_state() + _init_state())
    o_ref[...] = _finish_pair([carry[0:3], carry[3:6]])


def _dsa_attn_prompt(q, kv, mask, slopes, batch, seq):
    nt = seq // TQ
    return pl.pallas_call(
        _dsa_attn_kernel,
        out_shape=jax.ShapeDtypeStruct((batch * seq, N_HEADS * HEAD_DIM), F32),
        grid=(batch, N_HEADS // 2, nt),
        in_specs=[
            pl.BlockSpec(memory_space=pltpu.SMEM),
            pl.BlockSpec((TQ, LANES), lambda b, hp, i: (b * nt + i, hp)),
            pl.BlockSpec((seq, LANES), lambda b, hp, i: (b, hp)),
            pl.BlockSpec((seq, LANES), lambda b, hp, i: (b, N_HEADS // 2 + hp)),
            pl.BlockSpec((TQ, seq), lambda b, hp, i: (b * nt + i, 0)),
        ],
        out_specs=pl.BlockSpec((TQ, LANES), lambda b, hp, i: (b * nt + i, hp)),
        compiler_params=_cparams(3),
        name="dsa_attn_prompt",
    )(slopes, q, kv, kv, mask)


N_CHUNK = N_PAGES + 1
ROWS = N_HEADS * DEC_SEQ


def _row_slopes(slopes_ref, n_heads, rows):
    r = lax.broadcasted_iota(I32, (rows, 1), 0) // DEC_SEQ
    out = jnp.zeros((rows, 1), F32)
    for h in range(n_heads):
        out = jnp.where(r == h, slopes_ref[h], out)
    return out


def _pad_new(x8):
    return jnp.concatenate([x8, jnp.zeros((PAGE_SIZE - DEC_SEQ, x8.shape[1]), x8.dtype)], axis=0)


def _diag_heads(acc, n_heads):
    lane = lax.broadcasted_iota(I32, (DEC_SEQ, acc.shape[1]), 1) // HEAD_DIM
    out = jnp.zeros((DEC_SEQ, acc.shape[1]), F32)
    for h in range(n_heads):
        out = jnp.where(lane == h, acc[h * DEC_SEQ:(h + 1) * DEC_SEQ, :], out)
    return out


def _even_decode_kernel(pt_ref, slopes_ref, qrows_ref, new_ref, *refs):
    page_refs = refs[:N_PAGES]
    o_ref = refs[N_PAGES]
    s_sc, p_sc, km_sc = refs[N_PAGES + 1:]
    half = H_MOBA * HEAD_DIM
    mrows = H_MOBA * DEC_SEQ
    qrows = qrows_ref[0]
    newkv = new_ref[...]

    def k_chunk(c):
        if c < N_PAGES:
            return page_refs[c][0, :, 0:D_MODEL]
        return _pad_new(newkv[:, 0:D_MODEL])

    def v_chunk(c, lo):
        if c < N_PAGES:
            return page_refs[c][0, :, D_MODEL + lo:D_MODEL + lo + half].astype(BF16)
        return _pad_new(newkv[:, D_MODEL + lo:D_MODEL + lo + half]).astype(BF16)

    km_sc[...] = jnp.zeros(km_sc.shape, F32)
    for c in range(N_CHUNK):
        k = k_chunk(c)
        s_sc[:, c * LANES:(c + 1) * LANES] = _dot_nt(qrows, k.astype(BF16)) * SCALE
        if c < N_PAGES:
            part = jnp.sum(k[:, 0:half], axis=0, keepdims=True)
            if c % 2 == 0:
                km_sc[c // 2:c // 2 + 1, :] = part
            else:
                km_sc[c // 2:c // 2 + 1, :] = (km_sc[c // 2:c // 2 + 1, :] + part) * (1.0 / MOBA_BLOCK)

    nb = N_PAGES // 2
    nb_pad = km_sc.shape[0]
    blk = lax.broadcasted_iota(I32, (mrows, nb_pad), 1)
    gate = jnp.where(blk < nb, _dot_nt(qrows[0:mrows, 0:half], km_sc[...].astype(BF16)), -jnp.inf)
    rank = jnp.zeros((mrows, nb_pad), I32)
    for n2 in range(nb):
        gn = gate[:, n2:n2 + 1]
        rank = rank + ((gn > gate) | ((gn == gate) & (n2 < blk))).astype(I32)
    sel = jnp.where((blk < nb) & (rank < MOBA_TOPK), 1.0, 0.0)
    slope = _row_slopes(slopes_ref, H_MOBA, mrows)
    qi = lax.broadcasted_iota(I32, (mrows, LANES), 0) % DEC_SEQ
    cl = lax.broadcasted_iota(I32, (mrows, LANES), 1)

    def moba_scores(c):
        s = s_sc[0:mrows, c * LANES:(c + 1) * LANES]
        if c < N_PAGES:
            dist = (PAST_LEN + qi - (c * PAGE_SIZE + cl)).astype(F32)
            keep = jnp.broadcast_to(sel[:, c // 2:c // 2 + 1], (mrows, LANES)) > 0.5
        else:
            dist = (qi - cl).astype(F32)
            keep = cl <= qi
        return jnp.where(keep, s - slope * dist, NEG_MASKED)

    m = jnp.full((mrows, 1), NEG_INIT, F32)
    for c in range(N_CHUNK):
        sc = moba_scores(c)
        p_sc[:, c * LANES:(c + 1) * LANES] = sc
        m = jnp.maximum(m, jnp.max(sc, axis=1, keepdims=True))
    l = jnp.zeros((mrows, 1), F32)
    for c in range(N_CHUNK):
        p = jnp.exp(p_sc[:, c * LANES:(c + 1) * LANES] - m)
        p_sc[:, c * LANES:(c + 1) * LANES] = p
        l = l + jnp.sum(p, axis=1, keepdims=True)
    inv = 1.0 / l
    acc = jnp.zeros((mrows, half), F32)
    for c in range(N_CHUNK):
        p = (p_sc[:, c * LANES:(c + 1) * LANES] * inv).astype(BF16)
        acc = acc + _dot(p, v_chunk(c, 0))
    o_ref[:, 0:half] = _diag_heads(acc, H_MOBA)

    r2 = lax.broadcasted_iota(I32, (LANES, LANES), 0)
    c2 = lax.broadcasted_iota(I32, (LANES, LANES), 1)
    tri = (r2 >= c2).astype(BF16)
    carry = jnp.zeros((mrows, 1), F32)
    acc = jnp.zeros((mrows, half), F32)
    for c in range(N_CHUNK - 1, -1, -1):
        z = s_sc[mrows:2 * mrows, c * LANES:(c + 1) * LANES]
        lk = _log_sigmoid_neg(z)
        if c == N_PAGES:
            strict = cl < qi
            lk = jnp.where(strict, lk, 0.0)
        incl = _split_dot(lk, tri)
        w = jnp.exp(z + incl + carry)
        if c == N_PAGES:
            w = jnp.where(strict, w, 0.0)
        carry = carry + incl[:, 0:1]
        acc = acc + _dot(w.astype(BF16), v_chunk(c, half))
    o_ref[:, half:2 * half] = _diag_heads(acc, N_HEADS - H_MOBA)


def _page_specs(n_tail):
    def spec(p):
        return pl.BlockSpec((1, PAGE_SIZE, n_tail), lambda s, pt: (pt[s, p], 0, 0))
    return [spec(p) for p in range(N_PAGES)]


def _even_decode(qrows, newkv, cache, page_table, slopes):
    n_seq = qrows.shape[0]
    width = N_CHUNK * LANES
    grid_spec = pltpu.PrefetchScalarGridSpec(
        num_scalar_prefetch=1,
        grid=(n_seq,),
        in_specs=[
            pl.BlockSpec(memory_space=pltpu.SMEM),
            pl.BlockSpec((1, ROWS, D_MODEL), lambda s, pt: (s, 0, 0)),
            pl.BlockSpec((DEC_SEQ, 2 * D_MODEL), lambda s, pt: (s, 0)),
        ] + _page_specs(2 * D_MODEL),
        out_specs=pl.BlockSpec((DEC_SEQ, D_MODEL), lambda s, pt: (s, 0)),
        scratch_shapes=[pltpu.VMEM((ROWS, width), F32),
                        pltpu.VMEM((H_MOBA * DEC_SEQ, width), F32),
                        pltpu.VMEM((2 * DEC_SEQ, H_MOBA * HEAD_DIM), F32)],
    )
    return pl.pallas_call(
        _even_decode_kernel,
        out_shape=jax.ShapeDtypeStruct((n_seq * DEC_SEQ, D_MODEL), F32),
        grid_spec=grid_spec,
        compiler_params=_cparams(1),
        name="even_decode",
    )(page_table, slopes, qrows, newkv, *([cache] * N_PAGES))


def _odd_decode_kernel(pt_ref, slopes_ref, qrows_ref, qirows_ref, wi_ref, new_ref, newidx_ref, *refs):
    page_refs = refs[:N_PAGES]
    idx_refs = refs[N_PAGES:2 * N_PAGES]
    o_ref = refs[2 * N_PAGES]
    key_sc, s_sc = refs[2 * N_PAGES + 1:]
    qrows = qrows_ref[0]
    qirows = qirows_ref[0]
    wi = wi_ref[...]
    newkv = new_ref[...]
    qi8 = lax.broadcasted_iota(I32, (DEC_SEQ, LANES), 0)
    cl8 = lax.broadcasted_iota(I32, (DEC_SEQ, LANES), 1)

    for c in range(N_CHUNK):
        if c < N_PAGES:
            kidx = idx_refs[c][0].astype(BF16)
        else:
            kidx = _pad_new(newidx_ref[...]).astype(BF16)
        dots = jnp.maximum(_dot_nt(qirows, kidx), 0.0)
        score = jnp.zeros((DEC_SEQ, LANES), F32)
        for h in range(IDX_HEADS):
            score = score + wi[:, h:h + 1] * dots[h * DEC_SEQ:(h + 1) * DEC_SEQ, :]
        key = _ordered_key(score)
        if c == N_PAGES:
            key = jnp.where(cl8 <= qi8, key, INT_MIN)
        key_sc[:, c * LANES:(c + 1) * LANES] = key

    def count(pred):
        return jnp.sum(jnp.where(pred(key_sc[...]), 1.0, 0.0), axis=1, keepdims=True)

    thr = _kth_largest_key(lambda cand: count(lambda k: k >= cand), DEC_SEQ, DSA_TOPK)
    need = DSA_TOPK - count(lambda k: k > thr)
    r2 = lax.broadcasted_iota(I32, (LANES, LANES), 0)
    c2 = lax.broadcasted_iota(I32, (LANES, LANES), 1)
    tri = (r2 <= c2).astype(BF16)
    slope = _row_slopes(slopes_ref, N_HEADS, ROWS)
    qi = lax.broadcasted_iota(I32, (ROWS, LANES), 0) % DEC_SEQ
    cl = lax.broadcasted_iota(I32, (ROWS, LANES), 1)

    run = jnp.zeros((DEC_SEQ, 1), F32)
    m = jnp.full((ROWS, 1), NEG_INIT, F32)
    for c in range(N_CHUNK):
        key = key_sc[:, c * LANES:(c + 1) * LANES]
        eq = key == thr
        eqf = jnp.where(eq, 1.0, 0.0)
        prefix = _dot(eqf.astype(BF16), tri) + run
        run = run + jnp.sum(eqf, axis=1, keepdims=True)
        keep8 = ((key > thr) | (eq & (prefix <= need))) & (key != INT_MIN)
        keep = jnp.concatenate([jnp.where(keep8, 1.0, 0.0)] * N_HEADS, axis=0) > 0.0
        if c < N_PAGES:
            k = page_refs[c][0, :, 0:D_MODEL].astype(BF16)
            dist = (PAST_LEN + qi - (c * PAGE_SIZE + cl)).astype(F32)
        else:
            k = _pad_new(newkv[:, 0:D_MODEL]).astype(BF16)
            dist = (qi - cl).astype(F32)
        s = jnp.where(keep, _dot_nt(qrows, k) * SCALE - slope * dist, NEG_MASKED)
        s_sc[:, c * LANES:(c + 1) * LANES] = s
        m = jnp.maximum(m, jnp.max(s, axis=1, keepdims=True))

    l = jnp.zeros((ROWS, 1), F32)
    for c in range(N_CHUNK):
        p = jnp.exp(s_sc[:, c * LANES:(c + 1) * LANES] - m)
        s_sc[:, c * LANES:(c + 1) * LANES] = p
        l = l + jnp.sum(p, axis=1, keepdims=True)
    inv = 1.0 / l
    acc = jnp.zeros((ROWS, D_MODEL), F32)
    for c in range(N_CHUNK):
        p = (s_sc[:, c * LANES:(c + 1) * LANES] * inv).astype(BF16)
        if c < N_PAGES:
            v = page_refs[c][0, :, D_MODEL:2 * D_MODEL].astype(BF16)
        else:
            v = _pad_new(newkv[:, D_MODEL:2 * D_MODEL]).astype(BF16)
        acc = acc + _dot(p, v)
    o_ref[...] = _diag_heads(acc, N_HEADS)


def _odd_decode(qrows, qirows, wi, newkv, newidx, cache, cache_idx, page_table, slopes):
    n_seq = qrows.shape[0]
    width = N_CHUNK * LANES
    grid_spec = pltpu.PrefetchScalarGridSpec(
        num_scalar_prefetch=1,
        grid=(n_seq,),
        in_specs=[
            pl.BlockSpec(memory_space=pltpu.SMEM),
            pl.BlockSpec((1, ROWS, D_MODEL), lambda s, pt: (s, 0, 0)),
            pl.BlockSpec((1, IDX_HEADS * DEC_SEQ, IDX_DIM), lambda s, pt: (s, 0, 0)),
            pl.BlockSpec((DEC_SEQ, LANES), lambda s, pt: (s, 0)),
            pl.BlockSpec((DEC_SEQ, 2 * D_MODEL), lambda s, pt: (s, 0)),
            pl.BlockSpec((DEC_SEQ, IDX_DIM), lambda s, pt: (s, 0)),
        ] + _page_specs(2 * D_MODEL) + _page_specs(IDX_DIM),
        out_specs=pl.BlockSpec((DEC_SEQ, D_MODEL), lambda s, pt: (s, 0)),
        scratch_shapes=[pltpu.VMEM((DEC_SEQ, width), I32),
                        pltpu.VMEM((ROWS, width), F32)],
    )
    return pl.pallas_call(
        _odd_decode_kernel,
        out_shape=jax.ShapeDtypeStruct((n_seq * DEC_SEQ, D_MODEL), F32),
        grid_spec=grid_spec,
        compiler_params=_cparams(1),
        name="odd_decode",
    )(page_table, slopes, qrows, qirows, wi, newkv, newidx, *([cache] * N_PAGES), *([cache_idx] * N_PAGES))


def _layer_norm(x, g, b):
    mu = jnp.mean(x, axis=-1, keepdims=True)
    xc = x - mu
    var = jnp.mean(xc * xc, axis=-1, keepdims=True)
    return xc * lax.rsqrt(var + LN_EPS) * g + b


def _mix_norm_kernel(x_ref, g_ref, b_ref, wr_ref, br_ref, *refs, n_mix):
    o_refs = refs[:n_mix]
    w_refs = refs[n_mix:2 * n_mix]
    y_ref, idx_ref, gate_ref = refs[2 * n_mix:]
    mix = _dot(o_refs[0][...].astype(BF16), w_refs[0][...])
    for o_ref, w_ref in zip(o_refs[1:], w_refs[1:]):
        mix = mix + _dot(o_ref[...].astype(BF16), w_ref[...])
    y = _layer_norm(DEEPNORM_ALPHA * x_ref[...] + mix, g_ref[...], b_ref[...])
    y_ref[...] = y

    logits = _dot(y.astype(BF16), wr_ref[...]) + br_ref[...]
    lane = lax.broadcasted_iota(I32, logits.shape, 1)
    lane_f = lane.astype(F32)
    idx_out = jnp.zeros(logits.shape, I32)
    val_out = jnp.zeros(logits.shape, F32)
    top = None
    den = jnp.zeros((logits.shape[0], 1), F32)
    for k in range(TOP_K):
        mx = jnp.max(logits, axis=1, keepdims=True)
        ix = jnp.min(jnp.where(logits == mx, lane_f, float(LANES)), axis=1, keepdims=True).astype(I32)
        if k == 0:
            top = mx
        e = jnp.exp(mx - top)
        den = den + e
        idx_out = jnp.where(lane == k, ix, idx_out)
        val_out = jnp.where(lane == k, e, val_out)
        logits = jnp.where(lane == ix, -jnp.inf, logits)
    idx_ref[...] = idx_out
    gate_ref[...] = val_out / den


def _mix_norm_route(x, os_, ws, g, b, wr, br, tm=ROW_TILE):
    n_tok = x.shape[0]
    n = len(os_)
    return pl.pallas_call(
        functools.partial(_mix_norm_kernel, n_mix=n),
        out_shape=[jax.ShapeDtypeStruct((n_tok, D_MODEL), F32),
                   jax.ShapeDtypeStruct((n_tok, LANES), I32),
                   jax.ShapeDtypeStruct((n_tok, LANES), F32)],
        grid=(n_tok // tm,),
        in_specs=[pl.BlockSpec((tm, D_MODEL), lambda i: (i, 0)),
                  pl.BlockSpec((1, D_MODEL), lambda i: (0, 0)),
                  pl.BlockSpec((1, D_MODEL), lambda i: (0, 0)),
                  pl.BlockSpec((D_MODEL, LANES), lambda i: (0, 0)),
                  pl.BlockSpec((1, LANES), lambda i: (0, 0))]
        + [pl.BlockSpec((tm, o.shape[1]), lambda i: (i, 0)) for o in os_]
        + [pl.BlockSpec(w.shape, lambda i: (0, 0)) for w in ws],
        out_specs=[pl.BlockSpec((tm, D_MODEL), lambda i: (i, 0)),
                   pl.BlockSpec((tm, LANES), lambda i: (i, 0)),
                   pl.BlockSpec((tm, LANES), lambda i: (i, 0))],
        compiler_params=_cparams(1),
        name="mix_norm_route",
    )(x, g, b, wr, br, *os_, *ws)


def _moe_kernel(be_ref, nb_ref, x_ref, gate_ref, wgu_ref, bgu_ref, wd_ref, bd_ref, y_ref):
    i = pl.program_id(0)

    @pl.when(i < nb_ref[0])
    def _():
        h = _dot(x_ref[...], wgu_ref[0]) + bgu_ref[0]
        glu = jnp.minimum(h[:, :D_FF], SWIGLU_LIMIT)
        lin = jnp.clip(h[:, D_FF:], -SWIGLU_LIMIT, SWIGLU_LIMIT)
        act = glu * jax.nn.sigmoid(SWIGLU_ALPHA * glu) * (lin + 1.0)
        y = _dot(act.astype(BF16), wd_ref[0]) + bd_ref[0]
        y_ref[...] = y * gate_ref[...]

    @pl.when(i >= nb_ref[0])
    def _():
        y_ref[...] = jnp.zeros(y_ref.shape, y_ref.dtype)


def _moe_blocks(x_sorted, slot_gate, block_expert, n_used, wgu, bgu, wd, bd):
    n_slots = x_sorted.shape[0]
    n_blocks = n_slots // MOE_BLOCK
    grid_spec = pltpu.PrefetchScalarGridSpec(
        num_scalar_prefetch=2,
        grid=(n_blocks,),
        in_specs=[
            pl.BlockSpec((MOE_BLOCK, D_MODEL), lambda i, be, nb: (i, 0)),
            pl.BlockSpec((MOE_BLOCK, 1), lambda i, be, nb: (i, 0)),
            pl.BlockSpec((1, D_MODEL, 2 * D_FF), lambda i, be, nb: (be[i], 0, 0)),
            pl.BlockSpec((1, 1, 2 * D_FF), lambda i, be, nb: (be[i], 0, 0)),
            pl.BlockSpec((1, D_FF, D_MODEL), lambda i, be, nb: (be[i], 0, 0)),
            pl.BlockSpec((1, 1, D_MODEL), lambda i, be, nb: (be[i], 0, 0)),
        ],
        out_specs=pl.BlockSpec((MOE_BLOCK, D_MODEL), lambda i, be, nb: (i, 0)),
    )
    return pl.pallas_call(
        _moe_kernel,
        out_shape=jax.ShapeDtypeStruct((n_slots, D_MODEL), F32),
        grid_spec=grid_spec,
        compiler_params=_cparams(1),
        name="moe_blocks",
    )(block_expert, n_used, x_sorted, slot_gate, wgu, bgu, wd, bd)


def _ffn_norm_kernel(x_ref, yg_ref, g_ref, b_ref, o_ref):
    ffn = yg_ref[0] + yg_ref[1] + yg_ref[2] + yg_ref[3]
    o_ref[...] = _layer_norm(DEEPNORM_ALPHA * x_ref[...] + ffn, g_ref[...], b_ref[...])


def _ffn_norm(x, yg, g, b, tm=ROW_TILE):
    n_tok = x.shape[0]
    return pl.pallas_call(
        _ffn_norm_kernel,
        out_shape=jax.ShapeDtypeStruct((n_tok, D_MODEL), F32),
        grid=(n_tok // tm,),
        in_specs=[pl.BlockSpec((tm, D_MODEL), lambda i: (i, 0)),
                  pl.BlockSpec((TOP_K, tm, D_MODEL), lambda i: (0, i, 0)),
                  pl.BlockSpec((1, D_MODEL), lambda i: (0, 0)),
                  pl.BlockSpec((1, D_MODEL), lambda i: (0, 0))],
        out_specs=pl.BlockSpec((tm, D_MODEL), lambda i: (i, 0)),
        compiler_params=_cparams(1),
        name="ffn_norm",
    )(x, yg, g, b)


def _moe_layer(x, top_idx, gate, wgu, bgu, wd, bd, g, b):
    n_tok = x.shape[0]
    n_assign = n_tok * TOP_K
    e_flat = top_idx.reshape(n_assign)
    onehot = (e_flat[:, None] == jnp.arange(N_EXPERTS, dtype=I32)[None, :]).astype(I32)
    csum = jnp.cumsum(onehot, axis=0)
    counts = csum[-1]
    pos = jnp.take_along_axis(csum, e_flat[:, None], axis=1)[:, 0] - 1
    padded = (counts + MOE_BLOCK - 1) // MOE_BLOCK * MOE_BLOCK
    pad_end = jnp.cumsum(padded)
    pad_start = pad_end - padded
    slot = pad_start[e_flat] + pos
    n_blocks = -(-n_assign // MOE_BLOCK) + N_EXPERTS
    n_slots = n_blocks * MOE_BLOCK
    tok_flat = jnp.repeat(jnp.arange(n_tok, dtype=I32), TOP_K)
    slot_tok = jnp.full((n_slots,), n_tok, I32).at[slot].set(tok_flat)
    slot_gate = jnp.zeros((n_slots,), F32).at[slot].set(gate.reshape(n_assign))
    n_used = (pad_end[-1] // MOE_BLOCK).astype(I32)
    blk_start = jnp.arange(n_blocks, dtype=I32) * MOE_BLOCK
    block_expert = jnp.minimum(jnp.searchsorted(pad_end, blk_start, side='right'), N_EXPERTS - 1).astype(I32)
    last_expert = block_expert[jnp.maximum(n_used - 1, 0)]
    block_expert = jnp.where(jnp.arange(n_blocks) < n_used, block_expert, last_expert)
    x_pad = jnp.concatenate([x.astype(BF16), jnp.zeros((1, D_MODEL), BF16)], axis=0)
    x_sorted = x_pad[slot_tok]
    y_slots = _moe_blocks(x_sorted, slot_gate[:, None], block_expert, n_used[None], wgu, bgu, wd, bd)
    yg = y_slots[slot.reshape(n_tok, TOP_K).T]
    return _ffn_norm(x, yg, g, b)


def _alibi_slopes(n_heads):
    return jnp.exp2(-8.0 * jnp.arange(1, n_heads + 1, dtype=F32) / n_heads)


def _block_diag_rows(q, n_heads, width):
    n_seq = q.shape[0] // DEC_SEQ
    q4 = q.reshape(n_seq, DEC_SEQ, n_heads, width)
    eye = jnp.eye(n_heads, dtype=q.dtype)
    out = jnp.einsum('sqhd,hg->shqgd', q4, eye)
    return out.reshape(n_seq, n_heads * DEC_SEQ, n_heads * width)


def kernel(x_prompt, x_sample, cache_kv_even, cache_kv_odd, cache_kidx_odd, page_table,
           w_in_even, w_o_even, w_in_odd, w_o_odd, ln_g, ln_b,
           w_router, b_router, w_gate_up, b_gate_up, w_down, b_down):
    batch, seq, _ = x_prompt.shape
    n_seq, dec_seq, _ = x_sample.shape
    n_p = batch * seq
    n_s = n_seq * dec_seq
    hd = N_HEADS * HEAD_DIM
    x = jnp.concatenate([x_prompt.reshape(n_p, D_MODEL), x_sample.reshape(n_s, D_MODEL)], axis=0)

    slopes_even = _alibi_slopes(H_MOBA)
    slopes_odd = _alibi_slopes(N_HEADS)
    wr = jnp.pad(w_router, ((0, 0), (0, 0), (0, LANES - N_EXPERTS))).astype(BF16)
    br = jnp.pad(b_router, ((0, 0), (0, LANES - N_EXPERTS)), constant_values=-jnp.inf)[:, None, :]
    ln_g4 = ln_g[:, :, None, :]
    ln_b4 = ln_b[:, :, None, :]

    def moe(layer, x_mid, top_idx, gate):
        return _moe_layer(x_mid, top_idx[:, :TOP_K], gate[:, :TOP_K],
                          w_gate_up[layer].astype(BF16), b_gate_up[layer][:, None, :],
                          w_down[layer].astype(BF16), b_down[layer][:, None, :],
                          ln_g4[layer, 1], ln_b4[layer, 1])

    w_in = w_in_even[0].astype(BF16)
    ws = [w_in[:, :hd], w_in[:, hd:]]
    q_p, kv_p = _proj(x, 0, n_p, ws)
    q_s, kv_s = _proj(x, n_p, n_s, ws)
    o_moba = _moba_prompt(q_p, kv_p, slopes_even, batch, seq)
    o_sb = _sb_prompt(q_p, kv_p, batch, seq)
    qrows = _block_diag_rows(q_s.astype(BF16), N_HEADS, HEAD_DIM)
    cache_e = cache_kv_even[0].reshape(-1, PAGE_SIZE, 2 * hd)
    o_s = _even_decode(qrows, kv_s, cache_e, page_table, slopes_even)
    o_all = jnp.concatenate([jnp.concatenate([o_moba, o_sb], axis=1), o_s], axis=0)
    w_o = w_o_even[0].astype(BF16)
    x, top_idx, gate = _mix_norm_route(x, [o_all], [w_o], ln_g4[0, 0], ln_b4[0, 0], wr[0], br[0])
    x = moe(0, x, top_idx, gate)
    new_kv_even_p = kv_p.reshape(1, batch, seq, 2, N_HEADS, HEAD_DIM)
    new_kv_even_s = kv_s.reshape(1, n_seq, dec_seq, 2, N_HEADS, HEAD_DIM)

    w_in = w_in_odd[0].astype(BF16)
    n_qkv = 3 * hd
    n_qi = IDX_HEADS * IDX_DIM
    w_kidx = w_in[:, n_qkv + n_qi:n_qkv + n_qi + IDX_DIM]
    w_wi = jnp.pad(w_in[:, n_qkv + n_qi + IDX_DIM:], ((0, 0), (0, LANES - IDX_HEADS)))
    ws = [w_in[:, :hd], w_in[:, hd:n_qkv], w_in[:, n_qkv:n_qkv + n_qi],
          jnp.concatenate([w_kidx, w_kidx], axis=1), w_wi]
    q_p, kv_p, qi_p, kidx2_p, wi_p = _proj(x, 0, n_p, ws)
    q_s, kv_s, qi_s, kidx2_s, wi_s = _proj(x, n_p, n_s, ws)
    mask = _dsa_select_prompt(qi_p, wi_p, kidx2_p, batch, seq)
    o_p = _dsa_attn_prompt(q_p, kv_p, mask, slopes_odd, batch, seq)
    qrows = _block_diag_rows(q_s.astype(BF16), N_HEADS, HEAD_DIM)
    qirows = qi_s.astype(BF16).reshape(n_seq, DEC_SEQ, IDX_HEADS, IDX_DIM).transpose(0, 2, 1, 3)
    qirows = qirows.reshape(n_seq, IDX_HEADS * DEC_SEQ, IDX_DIM)
    kidx_s = kidx2_s[:, :IDX_DIM]
    cache_o = cache_kv_odd[0].reshape(-1, PAGE_SIZE, 2 * hd)
    o_s = _odd_decode(qrows, qirows, wi_s, kv_s, kidx_s, cache_o, cache_kidx_odd[0], page_table, slopes_odd)
    o_all = jnp.concatenate([o_p, o_s], axis=0)
    w_o = w_o_odd[0].astype(BF16)
    x, top_idx, gate = _mix_norm_route(x, [o_all], [w_o], ln_g4[1, 0], ln_b4[1, 0], wr[1], br[1])
    x = moe(1, x, top_idx, gate)
    new_kv_odd_p = kv_p.reshape(1, batch, seq, 2, N_HEADS, HEAD_DIM)
    new_kv_odd_s = kv_s.reshape(1, n_seq, dec_seq, 2, N_HEADS, HEAD_DIM)
    new_kidx_p = kidx2_p[:, :IDX_DIM].reshape(1, batch, seq, IDX_DIM)
    new_kidx_s = kidx_s.reshape(1, n_seq, dec_seq, IDX_DIM)

    y_prompt = x[:n_p].reshape(batch, seq, D_MODEL)
    y_sample = x[n_p:].reshape(n_seq, dec_seq, D_MODEL)
    return (y_prompt, y_sample, new_kv_even_p, new_kv_even_s, new_kv_odd_p, new_kv_odd_s,
            new_kidx_p, new_kidx_s)
```

```python
import functools

import jax
import jax.numpy as jnp
from jax import lax
from jax.experimental import pallas as pl
from jax.experimental.pallas import tpu as pltpu

F32 = jnp.float32
BF16 = jnp.bfloat16
I32 = jnp.int32

D_MODEL = 1024
HEAD_DIM = 64
N_HEADS = 16
H_MOBA = 8
MOBA_BLOCK = 256
MOBA_TOPK = 3
DSA_TOPK = 256
IDX_HEADS = 8
IDX_DIM = 64
N_EXPERTS = 32
TOP_K = 4
D_FF = 1024
SWIGLU_LIMIT = 7.0
SWIGLU_ALPHA = 1.702
LN_EPS = 1e-5
DEPTH = 2
DEEPNORM_ALPHA = (2 * DEPTH) ** 0.25
PAGE_SIZE = 128
PAST_LEN = 2048
N_PAGES = PAST_LEN // PAGE_SIZE
DEC_SEQ = 8

LANES = 128
TQ = 256
MOE_BLOCK = 256
CAST_ROWS = 128
ROW_TILE = 512
NEG_MASKED = -2e30
NEG_INIT = -1e30
INT_MIN = -2 ** 31
SB_DEAD_TAIL = -104.0
VMEM_LIMIT = 56 * 1024 * 1024
SCALE = HEAD_DIM ** -0.5


def _cparams(n_grid):
    return pltpu.CompilerParams(dimension_semantics=("arbitrary",) * n_grid,
                                vmem_limit_bytes=VMEM_LIMIT)


def _dot(a, b):
    return jnp.dot(a, b, preferred_element_type=F32)


def _dot_nt(a, b):
    return lax.dot_general(a, b, (((1,), (1,)), ((), ())), preferred_element_type=F32)


def _split_dot(x, tri):
    hi = x.astype(BF16)
    lo = (x - hi.astype(F32)).astype(BF16)
    return _dot(hi, tri) + _dot(lo, tri)


def _log_sigmoid_neg(z):
    return -(jnp.maximum(z, 0.0) + jnp.log(1.0 + jnp.exp(-jnp.abs(z))))


def _proj_kernel(x_ref, *refs, n_out):
    xb = x_ref[...].astype(BF16)
    for w_ref, o_ref in zip(refs[:n_out], refs[n_out:]):
        o_ref[...] = _dot(xb, w_ref[...])


def _proj(x, row0, n_rows, ws, tm=ROW_TILE):
    assert row0 % tm == 0 and n_rows % tm == 0
    blk0 = row0 // tm
    n = len(ws)
    return pl.pallas_call(
        functools.partial(_proj_kernel, n_out=n),
        out_shape=[jax.ShapeDtypeStruct((n_rows, w.shape[1]), F32) for w in ws],
        grid=(n_rows // tm,),
        in_specs=[pl.BlockSpec((tm, D_MODEL), lambda i: (i + blk0, 0))]
        + [pl.BlockSpec(w.shape, lambda i: (0, 0)) for w in ws],
        out_specs=[pl.BlockSpec((tm, w.shape[1]), lambda i: (i, 0)) for w in ws],
        compiler_params=_cparams(1),
        name="proj",
    )(x, *ws)


def _head_queries(q, scale):
    lane = lax.broadcasted_iota(I32, q.shape, 1)
    qs = q * scale
    return [jnp.where((lane >= hh * HEAD_DIM) & (lane < (hh + 1) * HEAD_DIM), qs, 0.0).astype(BF16)
            for hh in range(2)]


def _softmax_tile(s, kv_b, state):
    m, l, acc = state
    m_new = jnp.maximum(m, jnp.max(s, axis=1, keepdims=True))
    p = jnp.exp(s - m_new)
    alpha = jnp.exp(m - m_new)
    l = alpha * l + jnp.sum(p, axis=1, keepdims=True)
    acc = alpha * acc + _dot(p.astype(BF16), kv_b)
    return m_new, l, acc


def _init_state():
    return (jnp.full((TQ, 1), NEG_INIT, F32), jnp.zeros((TQ, 1), F32), jnp.zeros((TQ, LANES), F32))


def _finish_pair(states):
    lane = lax.broadcasted_iota(I32, (TQ, LANES), 1)
    o0 = states[0][2] / states[0][1]
    o1 = states[1][2] / states[1][1]
    return jnp.where(lane < HEAD_DIM, o0, o1)


def _moba_kernel(slopes_ref, q_ref, k_ref, v_ref, o_ref, kmean_sc, *, nt):
    hp = pl.program_id(1)
    i = pl.program_id(2)

    @pl.when(i == 0)
    def _():
        for n in range(nt):
            kmean_sc[n:n + 1, :] = jnp.sum(k_ref[n * TQ:(n + 1) * TQ, :], axis=0, keepdims=True) * (1.0 / MOBA_BLOCK)

    q = q_ref[...]
    q_gate = _head_queries(q, 1.0)
    q_att = _head_queries(q, SCALE)
    km = kmean_sc[...].astype(BF16)
    blk = lax.broadcasted_iota(I32, (TQ, nt), 1)
    past = blk < i
    row = lax.broadcasted_iota(I32, (TQ, TQ), 0)
    col = lax.broadcasted_iota(I32, (TQ, TQ), 1)
    rel = (row - col).astype(F32)

    sels, slopes = [], []
    for hh in range(2):
        gate = jnp.where(past, _dot_nt(q_gate[hh], km), -jnp.inf)
        rank = jnp.zeros((TQ, nt), I32)
        for n2 in range(nt):
            gn = gate[:, n2:n2 + 1]
            rank = rank + ((gn > gate) | ((gn == gate) & (n2 < blk))).astype(I32)
        sels.append((past & (rank < MOBA_TOPK)).astype(F32))
        slopes.append(slopes_ref[hp * 2 + hh])

    start = pl.multiple_of(i * TQ, TQ)
    kb = k_ref[pl.ds(start, TQ), :].astype(BF16)
    vb = v_ref[pl.ds(start, TQ), :].astype(BF16)
    states = []
    for hh in range(2):
        s = _dot_nt(q_att[hh], kb) - slopes[hh] * rel
        s = jnp.where(col <= row, s, NEG_MASKED)
        states.append(_softmax_tile(s, vb, _init_state()))

    def body(j, carry):
        st = [carry[0:3], carry[3:6]]
        off = pl.multiple_of(j * TQ, TQ)
        kb = k_ref[pl.ds(off, TQ), :].astype(BF16)
        vb = v_ref[pl.ds(off, TQ), :].astype(BF16)
        base = ((i - j) * TQ).astype(F32)
        out = []
        for hh in range(2):
            sel_j = jnp.sum(jnp.where(blk == j, sels[hh], 0.0), axis=1, keepdims=True) > 0.0
            s = _dot_nt(q_att[hh], kb) - slopes[hh] * (rel + base)
            s = jnp.where(sel_j, s, NEG_MASKED)
            out.extend(_softmax_tile(s, vb, st[hh]))
        return tuple(out)

    carry = lax.fori_loop(0, i, body, tuple(states[0]) + tuple(states[1]))
    o_ref[...] = _finish_pair([carry[0:3], carry[3:6]])


def _moba_prompt(q, kv, slopes, batch, seq):
    nt = seq // TQ
    return pl.pallas_call(
        functools.partial(_moba_kernel, nt=nt),
        out_shape=jax.ShapeDtypeStruct((batch * seq, H_MOBA * HEAD_DIM), F32),
        grid=(batch, H_MOBA // 2, nt),
        in_specs=[
            pl.BlockSpec(memory_space=pltpu.SMEM),
            pl.BlockSpec((TQ, LANES), lambda b, hp, i: (b * nt + i, hp)),
            pl.BlockSpec((seq, LANES), lambda b, hp, i: (b, hp)),
            pl.BlockSpec((seq, LANES), lambda b, hp, i: (b, N_HEADS // 2 + hp)),
        ],
        out_specs=pl.BlockSpec((TQ, LANES), lambda b, hp, i: (b * nt + i, hp)),
        scratch_shapes=[pltpu.VMEM((nt, LANES), F32)],
        compiler_params=_cparams(3),
        name="moba_prompt",
    )(slopes, q, kv, kv)


def _sb_tile(qh, kb, vb, tri, carry, acc, strict_mask):
    z = _dot_nt(qh, kb)
    lk = _log_sigmoid_neg(z)
    if strict_mask is not None:
        lk = jnp.where(strict_mask, lk, 0.0)
    incl = _split_dot(lk, tri)
    w = jnp.exp(z + incl + carry)
    if strict_mask is not None:
        w = jnp.where(strict_mask, w, 0.0)
    return carry + incl[:, 0:1], acc + _dot(w.astype(BF16), vb)


def _sb_kernel(q_ref, k_ref, v_ref, o_ref):
    i = pl.program_id(2)
    q_att = _head_queries(q_ref[...], SCALE)
    row = lax.broadcasted_iota(I32, (TQ, TQ), 0)
    col = lax.broadcasted_iota(I32, (TQ, TQ), 1)
    tri = (row >= col).astype(BF16)
    strict = col < row

    start = pl.multiple_of(i * TQ, TQ)
    kb = k_ref[pl.ds(start, TQ), :].astype(BF16)
    vb = v_ref[pl.ds(start, TQ), :].astype(BF16)
    init = []
    for hh in range(2):
        init.extend(_sb_tile(q_att[hh], kb, vb, tri, jnp.zeros((TQ, 1), F32),
                             jnp.zeros((TQ, LANES), F32), strict))

    def live(tail0, tail1):
        return jnp.max(jnp.maximum(tail0, tail1)) > SB_DEAD_TAIL

    def cond(state):
        return (state[0] < i) & state[1]

    def body(state):
        t, carry = state[0], state[2:]
        j = i - 1 - t
        off = pl.multiple_of(j * TQ, TQ)
        kb = k_ref[pl.ds(off, TQ), :].astype(BF16)
        vb = v_ref[pl.ds(off, TQ), :].astype(BF16)
        out = []
        for hh in range(2):
            out.extend(_sb_tile(q_att[hh], kb, vb, tri, carry[2 * hh], carry[2 * hh + 1], None))
        return (t + 1, live(out[0], out[2])) + tuple(out)

    state = lax.while_loop(cond, body, (jnp.int32(0), live(init[0], init[2])) + tuple(init))
    lane = lax.broadcasted_iota(I32, (TQ, LANES), 1)
    o_ref[...] = jnp.where(lane < HEAD_DIM, state[3], state[5])


def _sb_prompt(q, kv, batch, seq):
    nt = seq // TQ
    n_sb = N_HEADS - H_MOBA
    hp0 = H_MOBA // 2
    return pl.pallas_call(
        _sb_kernel,
        out_shape=jax.ShapeDtypeStruct((batch * seq, n_sb * HEAD_DIM), F32),
        grid=(batch, n_sb // 2, nt),
        in_specs=[
            pl.BlockSpec((TQ, LANES), lambda b, hp, i: (b * nt + i, hp0 + hp)),
            pl.BlockSpec((seq, LANES), lambda b, hp, i: (b, hp0 + hp)),
            pl.BlockSpec((seq, LANES), lambda b, hp, i: (b, N_HEADS // 2 + hp0 + hp)),
        ],
        out_specs=pl.BlockSpec((TQ, LANES), lambda b, hp, i: (b * nt + i, hp)),
        compiler_params=_cparams(3),
        name="sb_prompt",
    )(q, kv, kv)


def _ordered_key(score):
    bits = lax.bitcast_convert_type(score + 0.0, I32)
    return jnp.where(bits < 0, bits ^ 0x7FFFFFFF, bits)


def _kth_largest_key(count_ge, rows, k):
    t = jnp.where(count_ge(jnp.zeros((rows, 1), I32)) >= k, 0, INT_MIN).astype(I32)

    def body(b, t):
        cand = t | (1 << (30 - b))
        return jnp.where(count_ge(cand) >= k, cand, t)

    return lax.fori_loop(0, 31, body, t)


def _dsa_select_kernel(qi_ref, wi_ref, kidx_ref, mask_ref, key_sc, *, nt):
    i = pl.program_id(1)
    qi = qi_ref[...]
    wi = wi_ref[...]
    row = lax.broadcasted_iota(I32, (TQ, TQ), 0)
    col = lax.broadcasted_iota(I32, (TQ, TQ), 1)
    qh = []
    for p in range(IDX_HEADS // 2):
        qh.extend(_head_queries(qi[:, p * LANES:(p + 1) * LANES], 1.0))
    wcol = [wi[:, h:h + 1] for h in range(IDX_HEADS)]

    def score_tile(j):
        off = pl.multiple_of(j * TQ, TQ)
        kb = kidx_ref[pl.ds(off, TQ), :].astype(BF16)
        score = jnp.zeros((TQ, TQ), F32)
        for h in range(IDX_HEADS):
            score = score + wcol[h] * jnp.maximum(_dot_nt(qh[h], kb), 0.0)
        return _ordered_key(score)

    def fill(j, c):
        off = pl.multiple_of(j * TQ, TQ)
        key_sc[:, pl.ds(off, TQ)] = score_tile(j)
        return c

    lax.fori_loop(0, i, fill, 0)
    start = pl.multiple_of(i * TQ, TQ)
    key_sc[:, pl.ds(start, TQ)] = jnp.where(col <= row, score_tile(i), INT_MIN)

    def count(pred):
        def body(j, acc):
            off = pl.multiple_of(j * TQ, TQ)
            hit = jnp.where(pred(key_sc[:, pl.ds(off, TQ)]), 1.0, 0.0)
            return acc + hit[:, :LANES] + hit[:, LANES:]
        acc = lax.fori_loop(0, i + 1, body, jnp.zeros((TQ, LANES), F32))
        return jnp.sum(acc, axis=1, keepdims=True)

    thr = _kth_largest_key(lambda cand: count(lambda k: k >= cand), TQ, DSA_TOPK)
    need = DSA_TOPK - count(lambda k: k > thr)
    tri = (row <= col).astype(BF16)

    mask_ref[...] = jnp.zeros(mask_ref.shape, mask_ref.dtype)

    def emit(j, run):
        off = pl.multiple_of(j * TQ, TQ)
        key = key_sc[:, pl.ds(off, TQ)]
        eq = key == thr
        eqf = jnp.where(eq, 1.0, 0.0)
        prefix = _dot(eqf.astype(BF16), tri) + run
        keep = ((key > thr) | (eq & (prefix <= need))) & (key != INT_MIN)
        mask_ref[:, pl.ds(off, TQ)] = jnp.where(keep, 1.0, 0.0).astype(mask_ref.dtype)
        return run + jnp.sum(eqf, axis=1, keepdims=True)

    lax.fori_loop(0, i + 1, emit, jnp.zeros((TQ, 1), F32))


def _dsa_select_prompt(qi, wi, kidx2, batch, seq):
    nt = seq // TQ
    return pl.pallas_call(
        functools.partial(_dsa_select_kernel, nt=nt),
        out_shape=jax.ShapeDtypeStruct((batch * seq, seq), BF16),
        grid=(batch, nt),
        in_specs=[
            pl.BlockSpec((TQ, IDX_HEADS * IDX_DIM), lambda b, i: (b * nt + i, 0)),
            pl.BlockSpec((TQ, LANES), lambda b, i: (b * nt + i, 0)),
            pl.BlockSpec((seq, LANES), lambda b, i: (b, 0)),
        ],
        out_specs=pl.BlockSpec((TQ, seq), lambda b, i: (b * nt + i, 0)),
        scratch_shapes=[pltpu.VMEM((TQ, seq), I32)],
        compiler_params=_cparams(2),
        name="dsa_select_prompt",
    )(qi, wi, kidx2)


def _dsa_attn_kernel(slopes_ref, q_ref, k_ref, v_ref, mask_ref, o_ref):
    hp = pl.program_id(1)
    i = pl.program_id(2)
    q_att = _head_queries(q_ref[...], SCALE)
    row = lax.broadcasted_iota(I32, (TQ, TQ), 0)
    col = lax.broadcasted_iota(I32, (TQ, TQ), 1)
    rel = (row - col).astype(F32)
    slopes = [slopes_ref[hp * 2 + hh] for hh in range(2)]

    def body(j, carry):
        st = [carry[0:3], carry[3:6]]
        off = pl.multiple_of(j * TQ, TQ)
        kb = k_ref[pl.ds(off, TQ), :].astype(BF16)
        vb = v_ref[pl.ds(off, TQ), :].astype(BF16)
        keep = mask_ref[:, pl.ds(off, TQ)] > 0
        base = ((i - j) * TQ).astype(F32)
        out = []
        for hh in range(2):
            s = _dot_nt(q_att[hh], kb) - slopes[hh] * (rel + base)
            s = jnp.where(keep, s, NEG_MASKED)
            out.extend(_softmax_tile(s, vb, st[hh]))
        return tuple(out)

    carry = lax.fori_loop(0, i + 1, body, _init_state() + _init_state())
    o_ref[...] = _finish_pair([carry[0:3], carry[3:6]])


def _dsa_attn_prompt(q, kv, mask, slopes, batch, seq):
    nt = seq // TQ
    return pl.pallas_call(
        _dsa_attn_kernel,
        out_shape=jax.ShapeDtypeStruct((batch * seq, N_HEADS * HEAD_DIM), F32),
        grid=(batch, N_HEADS // 2, nt),
        in_specs=[
            pl.BlockSpec(memory_space=pltpu.SMEM),
            pl.BlockSpec((TQ, LANES), lambda b, hp, i: (b * nt + i, hp)),
            pl.BlockSpec((seq, LANES), lambda b, hp, i: (b, hp)),
            pl.BlockSpec((seq, LANES), lambda b, hp, i: (b, N_HEADS // 2 + hp)),
            pl.BlockSpec((TQ, seq), lambda b, hp, i: (b * nt + i, 0)),
        ],
        out_specs=pl.BlockSpec((TQ, LANES), lambda b, hp, i: (b * nt + i, hp)),
        compiler_params=_cparams(3),
        name="dsa_attn_prompt",
    )(slopes, q, kv, kv, mask)


N_CHUNK = N_PAGES + 1
ROWS = N_HEADS * DEC_SEQ
N_SLOT = 8
PAGE_PITCH = 40


def _row_slopes(slopes_ref, n_heads, rows):
    r = lax.broadcasted_iota(I32, (rows, 1), 0) // DEC_SEQ
    out = jnp.zeros((rows, 1), F32)
    for h in range(n_heads):
        out = jnp.where(r == h, slopes_ref[h], out)
    return out


def _pad_new(x8):
    return jnp.concatenate([x8, jnp.zeros((PAGE_SIZE - DEC_SEQ, x8.shape[1]), x8.dtype)], axis=0)


def _diag_heads(acc, n_heads):
    lane = lax.broadcasted_iota(I32, (DEC_SEQ, acc.shape[1]), 1) // HEAD_DIM
    out = jnp.zeros((DEC_SEQ, acc.shape[1]), F32)
    for h in range(n_heads):
        out = jnp.where(lane == h, acc[h * DEC_SEQ:(h + 1) * DEC_SEQ, :], out)
    return out


def _page_copy(cache_ref, raw, sem, page, slot):
    return pltpu.make_async_copy(cache_ref.at[page], raw.at[slot, :, pl.ds(0, 2 * N_HEADS), :], sem.at[slot])


def _load_pages(pt_ref, cache_ref, raw, sem, kd, vd, on_k_pair=None):
    s = pl.program_id(0)
    n_seq = pl.num_programs(0)

    @pl.when(s == 0)
    def _():
        for p in range(N_SLOT):
            _page_copy(cache_ref, raw, sem, pt_ref[0, p], p).start()

    for p in range(N_PAGES):
        slot = p % N_SLOT
        _page_copy(cache_ref, raw, sem, 0, slot).wait()
        flat = raw.at[slot].reshape(PAGE_SIZE * PAGE_PITCH, HEAD_DIM)
        for c in range(2):
            dst = kd if c == 0 else vd
            for hp in range(N_HEADS // 2):
                a = flat[pl.ds(c * N_HEADS + 2 * hp, PAGE_SIZE, stride=PAGE_PITCH), :]
                b = flat[pl.ds(c * N_HEADS + 2 * hp + 1, PAGE_SIZE, stride=PAGE_PITCH), :]
                pair = jnp.concatenate([a, b], axis=1)
                if c == 0 and on_k_pair is not None:
                    on_k_pair(p, hp, pair)
                dst[p, :, hp * LANES:(hp + 1) * LANES] = pair.astype(BF16)
        s_next = s + (p + N_SLOT) // N_PAGES
        p_next = (p + N_SLOT) % N_PAGES

        @pl.when(s_next < n_seq)
        def _():
            _page_copy(cache_ref, raw, sem, pt_ref[s_next, p_next], slot).start()


def _even_decode_kernel(pt_ref, slopes_ref, qrows_ref, new_ref, cache_ref, o_ref,
                        s_sc, p_sc, km_sc, raw, kd, vd, sem):
    half = H_MOBA * HEAD_DIM
    mrows = H_MOBA * DEC_SEQ
    qrows = qrows_ref[0]
    newkv = new_ref[...]

    km_sc[...] = jnp.zeros(km_sc.shape, F32)

    def block_mean(p, hp, pair):
        if hp < H_MOBA // 2:
            blk, cols = p // 2, slice(hp * LANES, (hp + 1) * LANES)
            part = jnp.sum(pair, axis=0, keepdims=True)
            if p % 2 == 0:
                km_sc[blk:blk + 1, cols] = part
            else:
                km_sc[blk:blk + 1, cols] = (km_sc[blk:blk + 1, cols] + part) * (1.0 / MOBA_BLOCK)

    _load_pages(pt_ref, cache_ref, raw, sem, kd, vd, block_mean)

    def k_chunk(c):
        if c < N_PAGES:
            return kd[c]
        return _pad_new(newkv[:, 0:D_MODEL]).astype(BF16)

    def v_chunk(c, lo):
        if c < N_PAGES:
            return vd[c, :, lo:lo + half]
        return _pad_new(newkv[:, D_MODEL + lo:D_MODEL + lo + half]).astype(BF16)

    for c in range(N_CHUNK):
        s_sc[:, c * LANES:(c + 1) * LANES] = _dot_nt(qrows, k_chunk(c)) * SCALE

    nb = N_PAGES // 2
    nb_pad = km_sc.shape[0]
    blk = lax.broadcasted_iota(I32, (mrows, nb_pad), 1)
    gate = jnp.where(blk < nb, _dot_nt(qrows[0:mrows, 0:half], km_sc[...].astype(BF16)), -jnp.inf)
    rank = jnp.zeros((mrows, nb_pad), I32)
    for n2 in range(nb):
        gn = gate[:, n2:n2 + 1]
        rank = rank + ((gn > gate) | ((gn == gate) & (n2 < blk))).astype(I32)
    sel = jnp.where((blk < nb) & (rank < MOBA_TOPK), 1.0, 0.0)
    slope = _row_slopes(slopes_ref, H_MOBA, mrows)
    qi = lax.broadcasted_iota(I32, (mrows, LANES), 0) % DEC_SEQ
    cl = lax.broadcasted_iota(I32, (mrows, LANES), 1)

    def moba_scores(c):
        s = s_sc[0:mrows, c * LANES:(c + 1) * LANES]
        if c < N_PAGES:
            dist = (PAST_LEN + qi - (c * PAGE_SIZE + cl)).astype(F32)
            keep = jnp.broadcast_to(sel[:, c // 2:c // 2 + 1], (mrows, LANES)) > 0.5
        else:
            dist = (qi - cl).astype(F32)
            keep = cl <= qi
        return jnp.where(keep, s - slope * dist, NEG_MASKED)

    m = jnp.full((mrows, 1), NEG_INIT, F32)
    for c in range(N_CHUNK):
        sc = moba_scores(c)
        p_sc[:, c * LANES:(c + 1) * LANES] = sc
        m = jnp.maximum(m, jnp.max(sc, axis=1, keepdims=True))
    l = jnp.zeros((mrows, 1), F32)
    for c in range(N_CHUNK):
        p = jnp.exp(p_sc[:, c * LANES:(c + 1) * LANES] - m)
        p_sc[:, c * LANES:(c + 1) * LANES] = p
        l = l + jnp.sum(p, axis=1, keepdims=True)
    inv = 1.0 / l
    acc = jnp.zeros((mrows, half), F32)
    for c in range(N_CHUNK):
        p = (p_sc[:, c * LANES:(c + 1) * LANES] * inv).astype(BF16)
        acc = acc + _dot(p, v_chunk(c, 0))
    o_ref[:, 0:half] = _diag_heads(acc, H_MOBA)

    r2 = lax.broadcasted_iota(I32, (LANES, LANES), 0)
    c2 = lax.broadcasted_iota(I32, (LANES, LANES), 1)
    tri = (r2 >= c2).astype(BF16)
    carry = jnp.zeros((mrows, 1), F32)
    acc = jnp.zeros((mrows, half), F32)
    for c in range(N_CHUNK - 1, -1, -1):
        z = s_sc[mrows:2 * mrows, c * LANES:(c + 1) * LANES]
        lk = _log_sigmoid_neg(z)
        if c == N_PAGES:
            strict = cl < qi
            lk = jnp.where(strict, lk, 0.0)
        incl = _split_dot(lk, tri)
        w = jnp.exp(z + incl + carry)
        if c == N_PAGES:
            w = jnp.where(strict, w, 0.0)
        carry = carry + incl[:, 0:1]
        acc = acc + _dot(w.astype(BF16), v_chunk(c, half))
    o_ref[:, half:2 * half] = _diag_heads(acc, N_HEADS - H_MOBA)


def _page_specs(n_tail):
    def spec(p):
        return pl.BlockSpec((1, PAGE_SIZE, n_tail), lambda s, pt: (pt[s, p], 0, 0))
    return [spec(p) for p in range(N_PAGES)]


def _page_scratch():
    return [pltpu.VMEM((N_SLOT, PAGE_SIZE, PAGE_PITCH, HEAD_DIM), F32),
            pltpu.VMEM((N_PAGES, PAGE_SIZE, D_MODEL), BF16),
            pltpu.VMEM((N_PAGES, PAGE_SIZE, D_MODEL), BF16),
            pltpu.SemaphoreType.DMA((N_SLOT,))]


def _even_decode(qrows, newkv, cache, page_table, slopes):
    n_seq = qrows.shape[0]
    width = N_CHUNK * LANES
    grid_spec = pltpu.PrefetchScalarGridSpec(
        num_scalar_prefetch=1,
        grid=(n_seq,),
        in_specs=[
            pl.BlockSpec(memory_space=pltpu.SMEM),
            pl.BlockSpec((1, ROWS, D_MODEL), lambda s, pt: (s, 0, 0)),
            pl.BlockSpec((DEC_SEQ, 2 * D_MODEL), lambda s, pt: (s, 0)),
            pl.BlockSpec(memory_space=pl.ANY),
        ],
        out_specs=pl.BlockSpec((DEC_SEQ, D_MODEL), lambda s, pt: (s, 0)),
        scratch_shapes=[pltpu.VMEM((ROWS, width), F32),
                        pltpu.VMEM((H_MOBA * DEC_SEQ, width), F32),
                        pltpu.VMEM((2 * DEC_SEQ, H_MOBA * HEAD_DIM), F32)]
        + _page_scratch(),
    )
    return pl.pallas_call(
        _even_decode_kernel,
        out_shape=jax.ShapeDtypeStruct((n_seq * DEC_SEQ, D_MODEL), F32),
        grid_spec=grid_spec,
        compiler_params=_cparams(1),
        name="even_decode",
    )(page_table, slopes, qrows, newkv, cache)


def _odd_decode_kernel(pt_ref, slopes_ref, qrows_ref, qirows_ref, wi_ref, new_ref, newidx_ref, cache_ref, *refs):
    idx_refs = refs[:N_PAGES]
    o_ref = refs[N_PAGES]
    key_sc, s_sc, raw, kd, vd, sem = refs[N_PAGES + 1:]
    _load_pages(pt_ref, cache_ref, raw, sem, kd, vd)
    qrows = qrows_ref[0]
    qirows = qirows_ref[0]
    wi = wi_ref[...]
    newkv = new_ref[...]
    qi8 = lax.broadcasted_iota(I32, (DEC_SEQ, LANES), 0)
    cl8 = lax.broadcasted_iota(I32, (DEC_SEQ, LANES), 1)

    for c in range(N_CHUNK):
        if c < N_PAGES:
            kidx = idx_refs[c][0].astype(BF16)
        else:
            kidx = _pad_new(newidx_ref[...]).astype(BF16)
        dots = jnp.maximum(_dot_nt(qirows, kidx), 0.0)
        score = jnp.zeros((DEC_SEQ, LANES), F32)
        for h in range(IDX_HEADS):
            score = score + wi[:, h:h + 1] * dots[h * DEC_SEQ:(h + 1) * DEC_SEQ, :]
        key = _ordered_key(score)
        if c == N_PAGES:
            key = jnp.where(cl8 <= qi8, key, INT_MIN)
        key_sc[:, c * LANES:(c + 1) * LANES] = key

    def count(pred):
        return jnp.sum(jnp.where(pred(key_sc[...]), 1.0, 0.0), axis=1, keepdims=True)

    thr = _kth_largest_key(lambda cand: count(lambda k: k >= cand), DEC_SEQ, DSA_TOPK)
    need = DSA_TOPK - count(lambda k: k > thr)
    r2 = lax.broadcasted_iota(I32, (LANES, LANES), 0)
    c2 = lax.broadcasted_iota(I32, (LANES, LANES), 1)
    tri = (r2 <= c2).astype(BF16)
    slope = _row_slopes(slopes_ref, N_HEADS, ROWS)
    qi = lax.broadcasted_iota(I32, (ROWS, LANES), 0) % DEC_SEQ
    cl = lax.broadcasted_iota(I32, (ROWS, LANES), 1)

    run = jnp.zeros((DEC_SEQ, 1), F32)
    m = jnp.full((ROWS, 1), NEG_INIT, F32)
    for c in range(N_CHUNK):
        key = key_sc[:, c * LANES:(c + 1) * LANES]
        eq = key == thr
        eqf = jnp.where(eq, 1.0, 0.0)
        prefix = _dot(eqf.astype(BF16), tri) + run
        run = run + jnp.sum(eqf, axis=1, keepdims=True)
        keep8 = ((key > thr) | (eq & (prefix <= need))) & (key != INT_MIN)
        keep = jnp.concatenate([jnp.where(keep8, 1.0, 0.0)] * N_HEADS, axis=0) > 0.0
        if c < N_PAGES:
            k = kd[c]
            dist = (PAST_LEN + qi - (c * PAGE_SIZE + cl)).astype(F32)
        else:
            k = _pad_new(newkv[:, 0:D_MODEL]).astype(BF16)
            dist = (qi - cl).astype(F32)
        s = jnp.where(keep, _dot_nt(qrows, k) * SCALE - slope * dist, NEG_MASKED)
        s_sc[:, c * LANES:(c + 1) * LANES] = s
        m = jnp.maximum(m, jnp.max(s, axis=1, keepdims=True))

    l = jnp.zeros((ROWS, 1), F32)
    for c in range(N_CHUNK):
        p = jnp.exp(s_sc[:, c * LANES:(c + 1) * LANES] - m)
        s_sc[:, c * LANES:(c + 1) * LANES] = p
        l = l + jnp.sum(p, axis=1, keepdims=True)
    inv = 1.0 / l
    acc = jnp.zeros((ROWS, D_MODEL), F32)
    for c in range(N_CHUNK):
        p = (s_sc[:, c * LANES:(c + 1) * LANES] * inv).astype(BF16)
        if c < N_PAGES:
            v = vd[c]
        else:
            v = _pad_new(newkv[:, D_MODEL:2 * D_MODEL]).astype(BF16)
        acc = acc + _dot(p, v)
    o_ref[...] = _diag_heads(acc, N_HEADS)


def _odd_decode(qrows, qirows, wi, newkv, newidx, cache, cache_idx, page_table, slopes):
    n_seq = qrows.shape[0]
    width = N_CHUNK * LANES
    grid_spec = pltpu.PrefetchScalarGridSpec(
        num_scalar_prefetch=1,
        grid=(n_seq,),
        in_specs=[
            pl.BlockSpec(memory_space=pltpu.SMEM),
            pl.BlockSpec((1, ROWS, D_MODEL), lambda s, pt: (s, 0, 0)),
            pl.BlockSpec((1, IDX_HEADS * DEC_SEQ, IDX_DIM), lambda s, pt: (s, 0, 0)),
            pl.BlockSpec((DEC_SEQ, LANES), lambda s, pt: (s, 0)),
            pl.BlockSpec((DEC_SEQ, 2 * D_MODEL), lambda s, pt: (s, 0)),
            pl.BlockSpec((DEC_SEQ, IDX_DIM), lambda s, pt: (s, 0)),
            pl.BlockSpec(memory_space=pl.ANY),
        ] + _page_specs(IDX_DIM),
        out_specs=pl.BlockSpec((DEC_SEQ, D_MODEL), lambda s, pt: (s, 0)),
        scratch_shapes=[pltpu.VMEM((DEC_SEQ, width), I32),
                        pltpu.VMEM((ROWS, width), F32)] + _page_scratch(),
    )
    return pl.pallas_call(
        _odd_decode_kernel,
        out_shape=jax.ShapeDtypeStruct((n_seq * DEC_SEQ, D_MODEL), F32),
        grid_spec=grid_spec,
        compiler_params=_cparams(1),
        name="odd_decode",
    )(page_table, slopes, qrows, qirows, wi, newkv, newidx, cache, *([cache_idx] * N_PAGES))


def _layer_norm(x, g, b):
    mu = jnp.mean(x, axis=-1, keepdims=True)
    xc = x - mu
    var = jnp.mean(xc * xc, axis=-1, keepdims=True)
    return xc * lax.rsqrt(var + LN_EPS) * g + b


def _mix_norm_kernel(x_ref, g_ref, b_ref, wr_ref, br_ref, *refs, n_mix):
    o_refs = refs[:n_mix]
    w_refs = refs[n_mix:2 * n_mix]
    y_ref, idx_ref, gate_ref = refs[2 * n_mix:]
    mix = _dot(o_refs[0][...].astype(BF16), w_refs[0][...])
    for o_ref, w_ref in zip(o_refs[1:], w_refs[1:]):
        mix = mix + _dot(o_ref[...].astype(BF16), w_ref[...])
    y = _layer_norm(DEEPNORM_ALPHA * x_ref[...] + mix, g_ref[...], b_ref[...])
    y_ref[...] = y

    logits = _dot(y.astype(BF16), wr_ref[...]) + br_ref[...]
    lane = lax.broadcasted_iota(I32, logits.shape, 1)
    lane_f = lane.astype(F32)
    idx_out = jnp.zeros(logits.shape, I32)
    val_out = jnp.zeros(logits.shape, F32)
    top = None
    den = jnp.zeros((logits.shape[0], 1), F32)
    for k in range(TOP_K):
        mx = jnp.max(logits, axis=1, keepdims=True)
        ix = jnp.min(jnp.where(logits == mx, lane_f, float(LANES)), axis=1, keepdims=True).astype(I32)
        if k == 0:
            top = mx
        e = jnp.exp(mx - top)
        den = den + e
        idx_out = jnp.where(lane == k, ix, idx_out)
        val_out = jnp.where(lane == k, e, val_out)
        logits = jnp.where(lane == ix, -jnp.inf, logits)
    idx_ref[...] = idx_out
    gate_ref[...] = val_out / den


def _mix_norm_route(x, os_, ws, g, b, wr, br, tm=ROW_TILE):
    n_tok = x.shape[0]
    n = len(os_)
    return pl.pallas_call(
        functools.partial(_mix_norm_kernel, n_mix=n),
        out_shape=[jax.ShapeDtypeStruct((n_tok, D_MODEL), F32),
                   jax.ShapeDtypeStruct((n_tok, LANES), I32),
                   jax.ShapeDtypeStruct((n_tok, LANES), F32)],
        grid=(n_tok // tm,),
        in_specs=[pl.BlockSpec((tm, D_MODEL), lambda i: (i, 0)),
                  pl.BlockSpec((1, D_MODEL), lambda i: (0, 0)),
                  pl.BlockSpec((1, D_MODEL), lambda i: (0, 0)),
                  pl.BlockSpec((D_MODEL, LANES), lambda i: (0, 0)),
                  pl.BlockSpec((1, LANES), lambda i: (0, 0))]
        + [pl.BlockSpec((tm, o.shape[1]), lambda i: (i, 0)) for o in os_]
        + [pl.BlockSpec(w.shape, lambda i: (0, 0)) for w in ws],
        out_specs=[pl.BlockSpec((tm, D_MODEL), lambda i: (i, 0)),
                   pl.BlockSpec((tm, LANES), lambda i: (i, 0)),
                   pl.BlockSpec((tm, LANES), lambda i: (i, 0))],
        compiler_params=_cparams(1),
        name="mix_norm_route",
    )(x, g, b, wr, br, *os_, *ws)


def _moe_kernel(be_ref, nb_ref, x_ref, gate_ref, wgu_ref, bgu_ref, wd_ref, bd_ref, y_ref, wgu_sc, wd_sc):
    i = pl.program_id(0)
    used = i < nb_ref[0]
    new_expert = (i == 0) | (be_ref[i] != be_ref[jnp.maximum(i - 1, 0)])

    @pl.when(used & new_expert)
    def _():
        for r in range(0, D_MODEL, CAST_ROWS):
            wgu_sc[r:r + CAST_ROWS, :] = wgu_ref[0, r:r + CAST_ROWS, :].astype(BF16)
        for r in range(0, D_FF, CAST_ROWS):
            wd_sc[r:r + CAST_ROWS, :] = wd_ref[0, r:r + CAST_ROWS, :].astype(BF16)

    @pl.when(used)
    def _():
        h = _dot(x_ref[...].astype(BF16), wgu_sc[...]) + bgu_ref[0]
        glu = jnp.minimum(h[:, :D_FF], SWIGLU_LIMIT)
        lin = jnp.clip(h[:, D_FF:], -SWIGLU_LIMIT, SWIGLU_LIMIT)
        act = glu * jax.nn.sigmoid(SWIGLU_ALPHA * glu) * (lin + 1.0)
        y = _dot(act.astype(BF16), wd_sc[...]) + bd_ref[0]
        y_ref[...] = y * gate_ref[...]

    @pl.when(i >= nb_ref[0])
    def _():
        y_ref[...] = jnp.zeros(y_ref.shape, y_ref.dtype)


def _moe_blocks(x_sorted, slot_gate, block_expert, n_used, wgu, bgu, wd, bd):
    n_slots = x_sorted.shape[0]
    n_blocks = n_slots // MOE_BLOCK
    grid_spec = pltpu.PrefetchScalarGridSpec(
        num_scalar_prefetch=2,
        grid=(n_blocks,),
        in_specs=[
            pl.BlockSpec((MOE_BLOCK, D_MODEL), lambda i, be, nb: (i, 0)),
            pl.BlockSpec((MOE_BLOCK, 1), lambda i, be, nb: (i, 0)),
            pl.BlockSpec((1, D_MODEL, 2 * D_FF), lambda i, be, nb: (be[i], 0, 0)),
            pl.BlockSpec((1, 1, 2 * D_FF), lambda i, be, nb: (be[i], 0, 0)),
            pl.BlockSpec((1, D_FF, D_MODEL), lambda i, be, nb: (be[i], 0, 0)),
            pl.BlockSpec((1, 1, D_MODEL), lambda i, be, nb: (be[i], 0, 0)),
        ],
        out_specs=pl.BlockSpec((MOE_BLOCK, D_MODEL), lambda i, be, nb: (i, 0)),
        scratch_shapes=[pltpu.VMEM((D_MODEL, 2 * D_FF), BF16), pltpu.VMEM((D_FF, D_MODEL), BF16)],
    )
    return pl.pallas_call(
        _moe_kernel,
        out_shape=jax.ShapeDtypeStruct((n_slots, D_MODEL), F32),
        grid_spec=grid_spec,
        compiler_params=_cparams(1),
        name="moe_blocks",
    )(block_expert, n_used, x_sorted, slot_gate, wgu, bgu, wd, bd)


def _ffn_norm_kernel(x_ref, yg_ref, g_ref, b_ref, o_ref):
    ffn = yg_ref[0] + yg_ref[1] + yg_ref[2] + yg_ref[3]
    o_ref[...] = _layer_norm(DEEPNORM_ALPHA * x_ref[...] + ffn, g_ref[...], b_ref[...])


def _ffn_norm(x, yg, g, b, tm=ROW_TILE):
    n_tok = x.shape[0]
    return pl.pallas_call(
        _ffn_norm_kernel,
        out_shape=jax.ShapeDtypeStruct((n_tok, D_MODEL), F32),
        grid=(n_tok // tm,),
        in_specs=[pl.BlockSpec((tm, D_MODEL), lambda i: (i, 0)),
                  pl.BlockSpec((TOP_K, tm, D_MODEL), lambda i: (0, i, 0)),
                  pl.BlockSpec((1, D_MODEL), lambda i: (0, 0)),
                  pl.BlockSpec((1, D_MODEL), lambda i: (0, 0))],
        out_specs=pl.BlockSpec((tm, D_MODEL), lambda i: (i, 0)),
        compiler_params=_cparams(1),
        name="ffn_norm",
    )(x, yg, g, b)


def _moe_layer(x, top_idx, gate, wgu, bgu, wd, bd, g, b):
    n_tok = x.shape[0]
    n_assign = n_tok * TOP_K
    e_flat = top_idx.reshape(n_assign)
    onehot = (e_flat[:, None] == jnp.arange(N_EXPERTS, dtype=I32)[None, :]).astype(I32)
    csum = jnp.cumsum(onehot, axis=0)
    counts = csum[-1]
    pos = jnp.take_along_axis(csum, e_flat[:, None], axis=1)[:, 0] - 1
    padded = (counts + MOE_BLOCK - 1) // MOE_BLOCK * MOE_BLOCK
    pad_end = jnp.cumsum(padded)
    pad_start = pad_end - padded
    slot = pad_start[e_flat] + pos
    n_blocks = -(-n_assign // MOE_BLOCK) + N_EXPERTS
    n_slots = n_blocks * MOE_BLOCK
    slot_assign = jnp.full((n_slots,), n_assign, I32).at[slot].set(jnp.arange(n_assign, dtype=I32))
    slot_tok = jnp.minimum(slot_assign // TOP_K, n_tok)
    slot_gate = jnp.concatenate([gate.reshape(n_assign), jnp.zeros((1,), F32)])[slot_assign]
    n_used = (pad_end[-1] // MOE_BLOCK).astype(I32)
    blk_start = jnp.arange(n_blocks, dtype=I32) * MOE_BLOCK
    block_expert = jnp.minimum(jnp.searchsorted(pad_end, blk_start, side='right'), N_EXPERTS - 1).astype(I32)
    last_expert = block_expert[jnp.maximum(n_used - 1, 0)]
    block_expert = jnp.where(jnp.arange(n_blocks) < n_used, block_expert, last_expert)
    x_pad = jnp.concatenate([x, jnp.zeros((1, D_MODEL), F32)], axis=0)
    x_sorted = x_pad[slot_tok]
    y_slots = _moe_blocks(x_sorted, slot_gate[:, None], block_expert, n_used[None], wgu, bgu, wd, bd)
    yg = y_slots[slot.reshape(n_tok, TOP_K).T]
    return _ffn_norm(x, yg, g, b)


def _alibi_slopes(n_heads):
    return jnp.exp2(-8.0 * jnp.arange(1, n_heads + 1, dtype=F32) / n_heads)


def _block_diag_rows(q, n_heads, width):
    n_seq = q.shape[0] // DEC_SEQ
    q4 = q.reshape(n_seq, DEC_SEQ, n_heads, width)
    eye = jnp.eye(n_heads, dtype=q.dtype)
    out = jnp.einsum('sqhd,hg->shqgd', q4, eye)
    return out.reshape(n_seq, n_heads * DEC_SEQ, n_heads * width)


def kernel(x_prompt, x_sample, cache_kv_even, cache_kv_odd, cache_kidx_odd, page_table,
           w_in_even, w_o_even, w_in_odd, w_o_odd, ln_g, ln_b,
           w_router, b_router, w_gate_up, b_gate_up, w_down, b_down):
    batch, seq, _ = x_prompt.shape
    n_seq, dec_seq, _ = x_sample.shape
    n_p = batch * seq
    n_s = n_seq * dec_seq
    hd = N_HEADS * HEAD_DIM
    x = jnp.concatenate([x_prompt.reshape(n_p, D_MODEL), x_sample.reshape(n_s, D_MODEL)], axis=0)

    slopes_even = _alibi_slopes(H_MOBA)
    slopes_odd = _alibi_slopes(N_HEADS)
    wr = jnp.pad(w_router, ((0, 0), (0, 0), (0, LANES - N_EXPERTS))).astype(BF16)
    br = jnp.pad(b_router, ((0, 0), (0, LANES - N_EXPERTS)), constant_values=-jnp.inf)[:, None, :]
    ln_g4 = ln_g[:, :, None, :]
    ln_b4 = ln_b[:, :, None, :]

    def moe(layer, x_mid, top_idx, gate):
        return _moe_layer(x_mid, top_idx[:, :TOP_K], gate[:, :TOP_K],
                          w_gate_up[layer], b_gate_up[layer][:, None, :],
                          w_down[layer], b_down[layer][:, None, :],
                          ln_g4[layer, 1], ln_b4[layer, 1])

    w_in = w_in_even[0].astype(BF16)
    ws = [w_in[:, :hd], w_in[:, hd:]]
    q_p, kv_p = _proj(x, 0, n_p, ws)
    q_s, kv_s = _proj(x, n_p, n_s, ws)
    o_moba = _moba_prompt(q_p, kv_p, slopes_even, batch, seq)
    o_sb = _sb_prompt(q_p, kv_p, batch, seq)
    qrows = _block_diag_rows(q_s.astype(BF16), N_HEADS, HEAD_DIM)
    cache_e = cache_kv_even[0].reshape(-1, PAGE_SIZE, 2 * N_HEADS, HEAD_DIM)
    o_s = _even_decode(qrows, kv_s, cache_e, page_table, slopes_even)
    o_all = jnp.concatenate([jnp.concatenate([o_moba, o_sb], axis=1), o_s], axis=0)
    w_o = w_o_even[0].astype(BF16)
    x, top_idx, gate = _mix_norm_route(x, [o_all], [w_o], ln_g4[0, 0], ln_b4[0, 0], wr[0], br[0])
    x = moe(0, x, top_idx, gate)
    new_kv_even_p = kv_p.reshape(1, batch, seq, 2, N_HEADS, HEAD_DIM)
    new_kv_even_s = kv_s.reshape(1, n_seq, dec_seq, 2, N_HEADS, HEAD_DIM)

    w_in = w_in_odd[0].astype(BF16)
    n_qkv = 3 * hd
    n_qi = IDX_HEADS * IDX_DIM
    w_kidx = w_in[:, n_qkv + n_qi:n_qkv + n_qi + IDX_DIM]
    w_wi = jnp.pad(w_in[:, n_qkv + n_qi + IDX_DIM:], ((0, 0), (0, LANES - IDX_HEADS)))
    ws = [w_in[:, :hd], w_in[:, hd:n_qkv], w_in[:, n_qkv:n_qkv + n_qi],
          jnp.concatenate([w_kidx, w_kidx], axis=1), w_wi]
    q_p, kv_p, qi_p, kidx2_p, wi_p = _proj(x, 0, n_p, ws)
    q_s, kv_s, qi_s, kidx2_s, wi_s = _proj(x, n_p, n_s, ws)
    mask = _dsa_select_prompt(qi_p, wi_p, kidx2_p, batch, seq)
    o_p = _dsa_attn_prompt(q_p, kv_p, mask, slopes_odd, batch, seq)
    qrows = _block_diag_rows(q_s.astype(BF16), N_HEADS, HEAD_DIM)
    qirows = qi_s.astype(BF16).reshape(n_seq, DEC_SEQ, IDX_HEADS, IDX_DIM).transpose(0, 2, 1, 3)
    qirows = qirows.reshape(n_seq, IDX_HEADS * DEC_SEQ, IDX_DIM)
    kidx_s = kidx2_s[:, :IDX_DIM]
    cache_o = cache_kv_odd[0].reshape(-1, PAGE_SIZE, 2 * N_HEADS, HEAD_DIM)
    o_s = _odd_decode(qrows, qirows, wi_s, kv_s, kidx_s, cache_o, cache_kidx_odd[0], page_table, slopes_odd)
    o_all = jnp.concatenate([o_p, o_s], axis=0)
    w_o = w_o_odd[0].astype(BF16)
    x, top_idx, gate = _mix_norm_route(x, [o_all], [w_o], ln_g4[1, 0], ln_b4[1, 0], wr[1], br[1])
    x = moe(1, x, top_idx, gate)
    new_kv_odd_p = kv_p.reshape(1, batch, seq, 2, N_HEADS, HEAD_DIM)
    new_kv_odd_s = kv_s.reshape(1, n_seq, dec_seq, 2, N_HEADS, HEAD_DIM)
    new_kidx_p = kidx2_p[:, :IDX_DIM].reshape(1, batch, seq, IDX_DIM)
    new_kidx_s = kidx_s.reshape(1, n_seq, dec_seq, IDX_DIM)

    y_prompt = x[:n_p].reshape(batch, seq, D_MODEL)
    y_sample = x[n_p:].reshape(n_seq, dec_seq, D_MODEL)
    return (y_prompt, y_sample, new_kv_even_p, new_kv_even_s, new_kv_odd_p, new_kv_odd_s,
            new_kidx_p, new_kidx_s)
```

```python
import functools

import jax
import jax.numpy as jnp
from jax import lax
from jax.experimental import pallas as pl
from jax.experimental.pallas import tpu as pltpu

F32 = jnp.float32
BF16 = jnp.bfloat16
I32 = jnp.int32

D_MODEL = 1024
HEAD_DIM = 64
N_HEADS = 16
H_MOBA = 8
MOBA_BLOCK = 256
MOBA_TOPK = 3
DSA_TOPK = 256
IDX_HEADS = 8
IDX_DIM = 64
N_EXPERTS = 32
TOP_K = 4
D_FF = 1024
SWIGLU_LIMIT = 7.0
SWIGLU_ALPHA = 1.702
LN_EPS = 1e-5
DEPTH = 2
DEEPNORM_ALPHA = (2 * DEPTH) ** 0.25
PAGE_SIZE = 128
PAST_LEN = 2048
N_PAGES = PAST_LEN // PAGE_SIZE
DEC_SEQ = 8

LANES = 128
TQ = 256
PAIRS = 2
MOE_BLOCK = 256
CAST_ROWS = 128
ROW_TILE = 512
NEG_MASKED = -2e30
NEG_INIT = -1e30
INT_MIN = -2 ** 31
SB_DEAD_TAIL = -104.0
VMEM_LIMIT = 56 * 1024 * 1024
SCALE = HEAD_DIM ** -0.5


def _cparams(n_grid):
    return pltpu.CompilerParams(dimension_semantics=("arbitrary",) * n_grid,
                                vmem_limit_bytes=VMEM_LIMIT)


def _dot(a, b):
    return jnp.dot(a, b, preferred_element_type=F32)


def _dot_nt(a, b):
    return lax.dot_general(a, b, (((1,), (1,)), ((), ())), preferred_element_type=F32)


def _split_dot(x, tri):
    hi = x.astype(BF16)
    lo = (x - hi.astype(F32)).astype(BF16)
    return _dot(hi, tri) + _dot(lo, tri)


def _log_sigmoid_neg(z):
    return -(jnp.maximum(z, 0.0) + jnp.log(1.0 + jnp.exp(-jnp.abs(z))))


def _proj_kernel(x_ref, *refs, n_out):
    xb = x_ref[...].astype(BF16)
    for w_ref, o_ref in zip(refs[:n_out], refs[n_out:]):
        o_ref[...] = _dot(xb, w_ref[...])


def _proj(x, row0, n_rows, ws, tm=ROW_TILE):
    assert row0 % tm == 0 and n_rows % tm == 0
    blk0 = row0 // tm
    n = len(ws)
    return pl.pallas_call(
        functools.partial(_proj_kernel, n_out=n),
        out_shape=[jax.ShapeDtypeStruct((n_rows, w.shape[1]), F32) for w in ws],
        grid=(n_rows // tm,),
        in_specs=[pl.BlockSpec((tm, D_MODEL), lambda i: (i + blk0, 0))]
        + [pl.BlockSpec(w.shape, lambda i: (0, 0)) for w in ws],
        out_specs=[pl.BlockSpec((tm, w.shape[1]), lambda i: (i, 0)) for w in ws],
        compiler_params=_cparams(1),
        name="proj",
    )(x, *ws)


def _head_queries(q, scale):
    lane = lax.broadcasted_iota(I32, q.shape, 1)
    qs = q * scale
    return [jnp.where((lane >= hh * HEAD_DIM) & (lane < (hh + 1) * HEAD_DIM), qs, 0.0).astype(BF16)
            for hh in range(2)]


def _softmax_tile(s, kv_b, state):
    m, l, acc = state
    m_new = jnp.maximum(m, jnp.max(s, axis=1, keepdims=True))
    p = jnp.exp(s - m_new)
    alpha = jnp.exp(m - m_new)
    l = alpha * l + jnp.sum(p, axis=1, keepdims=True)
    acc = alpha * acc + _dot(p.astype(BF16), kv_b)
    return m_new, l, acc


def _init_state():
    return (jnp.full((TQ, 1), NEG_INIT, F32), jnp.zeros((TQ, 1), F32), jnp.zeros((TQ, LANES), F32))


def _finish_pair(states):
    lane = lax.broadcasted_iota(I32, (TQ, LANES), 1)
    o0 = states[0][2] / states[0][1]
    o1 = states[1][2] / states[1][1]
    return jnp.where(lane < HEAD_DIM, o0, o1)


def _moba_kernel(slopes_ref, q_ref, k_ref, v_ref, o_ref, kmean_sc, *, nt):
    g = pl.program_id(1)
    i = pl.program_id(2)

    @pl.when(i == 0)
    def _():
        for n in range(nt):
            kmean_sc[n:n + 1, :] = jnp.sum(k_ref[n * TQ:(n + 1) * TQ, :], axis=0, keepdims=True) * (1.0 / MOBA_BLOCK)

    blk = lax.broadcasted_iota(I32, (TQ, nt), 1)
    past = blk < i
    row = lax.broadcasted_iota(I32, (TQ, TQ), 0)
    col = lax.broadcasted_iota(I32, (TQ, TQ), 1)
    rel = (row - col).astype(F32)

    q_att, sels, slopes = [], [], []
    for pr in range(PAIRS):
        q = q_ref[:, pr * LANES:(pr + 1) * LANES]
        km = kmean_sc[:, pr * LANES:(pr + 1) * LANES].astype(BF16)
        q_att.extend(_head_queries(q, SCALE))
        for hh, q_gate in enumerate(_head_queries(q, 1.0)):
            gate = jnp.where(past, _dot_nt(q_gate, km), -jnp.inf)
            rank = jnp.zeros((TQ, nt), I32)
            for n2 in range(nt):
                gn = gate[:, n2:n2 + 1]
                rank = rank + ((gn > gate) | ((gn == gate) & (n2 < blk))).astype(I32)
            sels.append((past & (rank < MOBA_TOPK)).astype(F32))
            slopes.append(slopes_ref[(g * PAIRS + pr) * 2 + hh])

    def kv_tile(off, pr):
        return (k_ref[pl.ds(off, TQ), pr * LANES:(pr + 1) * LANES].astype(BF16),
                v_ref[pl.ds(off, TQ), pr * LANES:(pr + 1) * LANES].astype(BF16))

    start = pl.multiple_of(i * TQ, TQ)
    init = []
    for pr in range(PAIRS):
        kb, vb = kv_tile(start, pr)
        for hh in range(2):
            s = _dot_nt(q_att[2 * pr + hh], kb) - slopes[2 * pr + hh] * rel
            s = jnp.where(col <= row, s, NEG_MASKED)
            init.extend(_softmax_tile(s, vb, _init_state()))

    def body(j, carry):
        off = pl.multiple_of(j * TQ, TQ)
        dist = rel + ((i - j) * TQ).astype(F32)
        out = []
        for pr in range(PAIRS):
            kb, vb = kv_tile(off, pr)
            for hh in range(2):
                h = 2 * pr + hh
                sel_j = jnp.sum(jnp.where(blk == j, sels[h], 0.0), axis=1, keepdims=True) > 0.0
                s = _dot_nt(q_att[h], kb) - slopes[h] * dist
                s = jnp.where(sel_j, s, NEG_MASKED)
                out.extend(_softmax_tile(s, vb, carry[3 * h:3 * h + 3]))
        return tuple(out)

    carry = lax.fori_loop(0, i, body, tuple(init))
    for pr in range(PAIRS):
        o_ref[:, pr * LANES:(pr + 1) * LANES] = _finish_pair([carry[6 * pr:6 * pr + 3], carry[6 * pr + 3:6 * pr + 6]])


def _moba_prompt(q, kv, slopes, batch, seq):
    nt = seq // TQ
    width = PAIRS * LANES
    v_blk0 = N_HEADS * HEAD_DIM // width
    return pl.pallas_call(
        functools.partial(_moba_kernel, nt=nt),
        out_shape=jax.ShapeDtypeStruct((batch * seq, H_MOBA * HEAD_DIM), F32),
        grid=(batch, H_MOBA // (2 * PAIRS), nt),
        in_specs=[
            pl.BlockSpec(memory_space=pltpu.SMEM),
            pl.BlockSpec((TQ, width), lambda b, g, i: (b * nt + i, g)),
            pl.BlockSpec((seq, width), lambda b, g, i: (b, g)),
            pl.BlockSpec((seq, width), lambda b, g, i: (b, v_blk0 + g)),
        ],
        out_specs=pl.BlockSpec((TQ, width), lambda b, g, i: (b * nt + i, g)),
        scratch_shapes=[pltpu.VMEM((nt, width), F32)],
        compiler_params=_cparams(3),
        name="moba_prompt",
    )(slopes, q, kv, kv)


def _sb_tile(qh, kb, vb, tri, carry, acc, strict_mask):
    z = _dot_nt(qh, kb)
    lk = _log_sigmoid_neg(z)
    if strict_mask is not None:
        lk = jnp.where(strict_mask, lk, 0.0)
    incl = _split_dot(lk, tri)
    w = jnp.exp(z + incl + carry)
    if strict_mask is not None:
        w = jnp.where(strict_mask, w, 0.0)
    return carry + incl[:, 0:1], acc + _dot(w.astype(BF16), vb)


def _sb_kernel(q_ref, k_ref, v_ref, o_ref):
    i = pl.program_id(2)
    q_att = _head_queries(q_ref[...], SCALE)
    row = lax.broadcasted_iota(I32, (TQ, TQ), 0)
    col = lax.broadcasted_iota(I32, (TQ, TQ), 1)
    tri = (row >= col).astype(BF16)
    strict = col < row

    start = pl.multiple_of(i * TQ, TQ)
    kb = k_ref[pl.ds(start, TQ), :].astype(BF16)
    vb = v_ref[pl.ds(start, TQ), :].astype(BF16)
    init = []
    for hh in range(2):
        init.extend(_sb_tile(q_att[hh], kb, vb, tri, jnp.zeros((TQ, 1), F32),
                             jnp.zeros((TQ, LANES), F32), strict))

    def live(tail0, tail1):
        return jnp.max(jnp.maximum(tail0, tail1)) > SB_DEAD_TAIL

    def cond(state):
        return (state[0] < i) & state[1]

    def body(state):
        t, carry = state[0], state[2:]
        j = i - 1 - t
        off = pl.multiple_of(j * TQ, TQ)
        kb = k_ref[pl.ds(off, TQ), :].astype(BF16)
        vb = v_ref[pl.ds(off, TQ), :].astype(BF16)
        out = []
        for hh in range(2):
            out.extend(_sb_tile(q_att[hh], kb, vb, tri, carry[2 * hh], carry[2 * hh + 1], None))
        return (t + 1, live(out[0], out[2])) + tuple(out)

    state = lax.while_loop(cond, body, (jnp.int32(0), live(init[0], init[2])) + tuple(init))
    lane = lax.broadcasted_iota(I32, (TQ, LANES), 1)
    o_ref[...] = jnp.where(lane < HEAD_DIM, state[3], state[5])


def _sb_prompt(q, kv, batch, seq):
    nt = seq // TQ
    n_sb = N_HEADS - H_MOBA
    hp0 = H_MOBA // 2
    return pl.pallas_call(
        _sb_kernel,
        out_shape=jax.ShapeDtypeStruct((batch * seq, n_sb * HEAD_DIM), F32),
        grid=(batch, n_sb // 2, nt),
        in_specs=[
            pl.BlockSpec((TQ, LANES), lambda b, hp, i: (b * nt + i, hp0 + hp)),
            pl.BlockSpec((seq, LANES), lambda b, hp, i: (b, hp0 + hp)),
            pl.BlockSpec((seq, LANES), lambda b, hp, i: (b, N_HEADS // 2 + hp0 + hp)),
        ],
        out_specs=pl.BlockSpec((TQ, LANES), lambda b, hp, i: (b * nt + i, hp)),
        compiler_params=_cparams(3),
        name="sb_prompt",
    )(q, kv, kv)


def _ordered_key(score):
    bits = lax.bitcast_convert_type(score + 0.0, I32)
    return jnp.where(bits < 0, bits ^ 0x7FFFFFFF, bits)


def _kth_largest_key(count_ge, rows, k):
    t = jnp.where(count_ge(jnp.zeros((rows, 1), I32)) >= k, 0, INT_MIN).astype(I32)

    def body(b, t):
        cand = t | (1 << (30 - b))
        return jnp.where(count_ge(cand) >= k, cand, t)

    return lax.fori_loop(0, 31, body, t)


def _dsa_select_kernel(qi_ref, wi_ref, kidx_ref, mask_ref, key_sc, *, nt):
    i = pl.program_id(1)
    qi = qi_ref[...]
    wi = wi_ref[...]
    row = lax.broadcasted_iota(I32, (TQ, TQ), 0)
    col = lax.broadcasted_iota(I32, (TQ, TQ), 1)
    qh = []
    for p in range(IDX_HEADS // 2):
        qh.extend(_head_queries(qi[:, p * LANES:(p + 1) * LANES], 1.0))
    wcol = [wi[:, h:h + 1] for h in range(IDX_HEADS)]

    def score_tile(j):
        off = pl.multiple_of(j * TQ, TQ)
        kb = kidx_ref[pl.ds(off, TQ), :].astype(BF16)
        score = jnp.zeros((TQ, TQ), F32)
        for h in range(IDX_HEADS):
            score = score + wcol[h] * jnp.maximum(_dot_nt(qh[h], kb), 0.0)
        return _ordered_key(score)

    def fill(j, c):
        off = pl.multiple_of(j * TQ, TQ)
        key_sc[:, pl.ds(off, TQ)] = score_tile(j)
        return c

    lax.fori_loop(0, i, fill, 0)
    start = pl.multiple_of(i * TQ, TQ)
    key_sc[:, pl.ds(start, TQ)] = jnp.where(col <= row, score_tile(i), INT_MIN)

    def count(pred):
        def body(j, acc):
            off = pl.multiple_of(j * TQ, TQ)
            hit = jnp.where(pred(key_sc[:, pl.ds(off, TQ)]), 1.0, 0.0)
            return acc + hit[:, :LANES] + hit[:, LANES:]
        acc = lax.fori_loop(0, i + 1, body, jnp.zeros((TQ, LANES), F32))
        return jnp.sum(acc, axis=1, keepdims=True)

    thr = _kth_largest_key(lambda cand: count(lambda k: k >= cand), TQ, DSA_TOPK)
    need = DSA_TOPK - count(lambda k: k > thr)
    tri = (row <= col).astype(BF16)

    mask_ref[...] = jnp.zeros(mask_ref.shape, mask_ref.dtype)

    def emit(j, run):
        off = pl.multiple_of(j * TQ, TQ)
        key = key_sc[:, pl.ds(off, TQ)]
        eq = key == thr
        eqf = jnp.where(eq, 1.0, 0.0)
        prefix = _dot(eqf.astype(BF16), tri) + run
        keep = ((key > thr) | (eq & (prefix <= need))) & (key != INT_MIN)
        mask_ref[:, pl.ds(off, TQ)] = jnp.where(keep, 1.0, 0.0).astype(mask_ref.dtype)
        return run + jnp.sum(eqf, axis=1, keepdims=True)

    lax.fori_loop(0, i + 1, emit, jnp.zeros((TQ, 1), F32))


def _dsa_select_prompt(qi, wi, kidx2, batch, seq):
    nt = seq // TQ
    return pl.pallas_call(
        functools.partial(_dsa_select_kernel, nt=nt),
        out_shape=jax.ShapeDtypeStruct((batch * seq, seq), BF16),
        grid=(batch, nt),
        in_specs=[
            pl.BlockSpec((TQ, IDX_HEADS * IDX_DIM), lambda b, i: (b * nt + i, 0)),
            pl.BlockSpec((TQ, LANES), lambda b, i: (b * nt + i, 0)),
            pl.BlockSpec((seq, LANES), lambda b, i: (b, 0)),
        ],
        out_specs=pl.BlockSpec((TQ, seq), lambda b, i: (b * nt + i, 0)),
        scratch_shapes=[pltpu.VMEM((TQ, seq), I32)],
        compiler_params=_cparams(2),
        name="dsa_select_prompt",
    )(qi, wi, kidx2)


def _dsa_attn_kernel(slopes_ref, q_ref, k_ref, v_ref, mask_ref, o_ref):
    g = pl.program_id(1)
    i = pl.program_id(2)
    row = lax.broadcasted_iota(I32, (TQ, TQ), 0)
    col = lax.broadcasted_iota(I32, (TQ, TQ), 1)
    rel = (row - col).astype(F32)
    q_att, slopes = [], []
    for pr in range(PAIRS):
        q_att.extend(_head_queries(q_ref[:, pr * LANES:(pr + 1) * LANES], SCALE))
        slopes.extend(slopes_ref[(g * PAIRS + pr) * 2 + hh] for hh in range(2))

    def body(j, carry):
        off = pl.multiple_of(j * TQ, TQ)
        keep = mask_ref[:, pl.ds(off, TQ)] > 0
        dist = rel + ((i - j) * TQ).astype(F32)
        out = []
        for pr in range(PAIRS):
            kb = k_ref[pl.ds(off, TQ), pr * LANES:(pr + 1) * LANES].astype(BF16)
            vb = v_ref[pl.ds(off, TQ), pr * LANES:(pr + 1) * LANES].astype(BF16)
            for hh in range(2):
                h = 2 * pr + hh
                s = _dot_nt(q_att[h], kb) - slopes[h] * dist
                s = jnp.where(keep, s, NEG_MASKED)
                out.extend(_softmax_tile(s, vb, carry[3 * h:3 * h + 3]))
        return tuple(out)

    carry = lax.fori_loop(0, i + 1, body, _init_state() * (2 * PAIRS))
    for pr in range(PAIRS):
        o_ref[:, pr * LANES:(pr + 1) * LANES] = _finish_pair([carry[6 * pr:6 * pr + 3], carry[6 * pr + 3:6 * pr + 6]])


def _dsa_attn_prompt(q, kv, mask, slopes, batch, seq):
    nt = seq // TQ
    width = PAIRS * LANES
    v_blk0 = N_HEADS * HEAD_DIM // width
    return pl.pallas_call(
        _dsa_attn_kernel,
        out_shape=jax.ShapeDtypeStruct((batch * seq, N_HEADS * HEAD_DIM), F32),
        grid=(batch, N_HEADS // (2 * PAIRS), nt),
        in_specs=[
            pl.BlockSpec(memory_space=pltpu.SMEM),
            pl.BlockSpec((TQ, width), lambda b, g, i: (b * nt + i, g)),
            pl.BlockSpec((seq, width), lambda b, g, i: (b, g)),
            pl.BlockSpec((seq, width), lambda b, g, i: (b, v_blk0 + g)),
            pl.BlockSpec((TQ, seq), lambda b, g, i: (b * nt + i, 0)),
        ],
        out_specs=pl.BlockSpec((TQ, width), lambda b, g, i: (b * nt + i, g)),
        compiler_params=_cparams(3),
        name="dsa_attn_prompt",
    )(slopes, q, kv, kv, mask)


N_CHUNK = N_PAGES + 1
ROWS = N_HEADS * DEC_SEQ


def _row_slopes(slopes_ref, n_heads, rows):
    r = lax.broadcasted_iota(I32, (rows, 1), 0) // DEC_SEQ
    out = jnp.zeros((rows, 1), F32)
    for h in range(n_heads):
        out = jnp.where(r == h, slopes_ref[h], out)
    return out


def _pad_new(x8):
    return jnp.concatenate([x8, jnp.zeros((PAGE_SIZE - DEC_SEQ, x8.shape[1]), x8.dtype)], axis=0)


def _diag_heads(acc, n_heads):
    lane = lax.broadcasted_iota(I32, (DEC_SEQ, acc.shape[1]), 1) // HEAD_DIM
    out = jnp.zeros((DEC_SEQ, acc.shape[1]), F32)
    for h in range(n_heads):
        out = jnp.where(lane == h, acc[h * DEC_SEQ:(h + 1) * DEC_SEQ, :], out)
    return out


def _scores(qrows, page_ref, new_rows):
    if page_ref is not None:
        return _dot(qrows, page_ref[0, 0].astype(BF16))
    return _dot_nt(qrows, _pad_new(new_rows).astype(BF16))


def _weighted_values(p, page_ref, new_rows, lo, n):
    if page_ref is not None:
        return _dot_nt(p, page_ref[0, 1, lo:lo + n, :].astype(BF16))
    return _dot(p, _pad_new(new_rows[:, lo:lo + n]).astype(BF16))


def _even_decode_kernel(pt_ref, slopes_ref, qrows_ref, new_ref, *refs):
    page_refs = list(refs[:N_PAGES]) + [None]
    o_ref = refs[N_PAGES]
    s_sc, p_sc, km_sc = refs[N_PAGES + 1:]
    half = H_MOBA * HEAD_DIM
    mrows = H_MOBA * DEC_SEQ
    qrows = qrows_ref[0]
    newkv = new_ref[...]
    new_k, new_v = newkv[:, 0:D_MODEL], newkv[:, D_MODEL:2 * D_MODEL]

    km_sc[...] = jnp.zeros(km_sc.shape, F32)
    for c in range(N_CHUNK):
        s_sc[:, c * LANES:(c + 1) * LANES] = _scores(qrows, page_refs[c], new_k) * SCALE
        if c < N_PAGES:
            part = jnp.sum(page_refs[c][0, 0, 0:half, :], axis=1, keepdims=True)
            n = c // 2
            if c % 2 == 0:
                km_sc[:, n:n + 1] = part
            else:
                km_sc[:, n:n + 1] = (km_sc[:, n:n + 1] + part) * (1.0 / MOBA_BLOCK)

    nb = N_PAGES // 2
    nb_pad = km_sc.shape[1]
    blk = lax.broadcasted_iota(I32, (mrows, nb_pad), 1)
    gate = jnp.where(blk < nb, _dot(qrows[0:mrows, 0:half], km_sc[...].astype(BF16)), -jnp.inf)
    rank = jnp.zeros((mrows, nb_pad), I32)
    for n2 in range(nb):
        gn = gate[:, n2:n2 + 1]
        rank = rank + ((gn > gate) | ((gn == gate) & (n2 < blk))).astype(I32)
    sel = jnp.where((blk < nb) & (rank < MOBA_TOPK), 1.0, 0.0)
    slope = _row_slopes(slopes_ref, H_MOBA, mrows)
    qi = lax.broadcasted_iota(I32, (mrows, LANES), 0) % DEC_SEQ
    cl = lax.broadcasted_iota(I32, (mrows, LANES), 1)

    def moba_scores(c):
        s = s_sc[0:mrows, c * LANES:(c + 1) * LANES]
        if c < N_PAGES:
            dist = (PAST_LEN + qi - (c * PAGE_SIZE + cl)).astype(F32)
            keep = jnp.broadcast_to(sel[:, c // 2:c // 2 + 1], (mrows, LANES)) > 0.5
        else:
            dist = (qi - cl).astype(F32)
            keep = cl <= qi
        return jnp.where(keep, s - slope * dist, NEG_MASKED)

    m = jnp.full((mrows, 1), NEG_INIT, F32)
    for c in range(N_CHUNK):
        sc = moba_scores(c)
        p_sc[:, c * LANES:(c + 1) * LANES] = sc
        m = jnp.maximum(m, jnp.max(sc, axis=1, keepdims=True))
    l = jnp.zeros((mrows, 1), F32)
    for c in range(N_CHUNK):
        p = jnp.exp(p_sc[:, c * LANES:(c + 1) * LANES] - m)
        p_sc[:, c * LANES:(c + 1) * LANES] = p
        l = l + jnp.sum(p, axis=1, keepdims=True)
    inv = 1.0 / l
    acc = jnp.zeros((mrows, half), F32)
    for c in range(N_CHUNK):
        p = (p_sc[:, c * LANES:(c + 1) * LANES] * inv).astype(BF16)
        acc = acc + _weighted_values(p, page_refs[c], new_v, 0, half)
    o_ref[:, 0:half] = _diag_heads(acc, H_MOBA)

    r2 = lax.broadcasted_iota(I32, (LANES, LANES), 0)
    c2 = lax.broadcasted_iota(I32, (LANES, LANES), 1)
    tri = (r2 >= c2).astype(BF16)
    carry = jnp.zeros((mrows, 1), F32)
    acc = jnp.zeros((mrows, half), F32)
    for c in range(N_CHUNK - 1, -1, -1):
        z = s_sc[mrows:2 * mrows, c * LANES:(c + 1) * LANES]
        lk = _log_sigmoid_neg(z)
        if c == N_PAGES:
            strict = cl < qi
            lk = jnp.where(strict, lk, 0.0)
        incl = _split_dot(lk, tri)
        w = jnp.exp(z + incl + carry)
        if c == N_PAGES:
            w = jnp.where(strict, w, 0.0)
        carry = carry + incl[:, 0:1]
        acc = acc + _weighted_values(w.astype(BF16), page_refs[c], new_v, half, half)
    o_ref[:, half:2 * half] = _diag_heads(acc, N_HEADS - H_MOBA)


def _page_specs(*tail):
    def spec(p):
        return pl.BlockSpec((1,) + tail + (PAGE_SIZE,), lambda s, pt: (pt[s, p],) + (0,) * (len(tail) + 1))
    return [spec(p) for p in range(N_PAGES)]


def _pool_pages_kv(cache):
    pool = jnp.transpose(cache[0], (0, 2, 3, 4, 1))
    return pool.reshape(pool.shape[0], 2, D_MODEL, PAGE_SIZE)


def _even_decode(qrows, newkv, pool, page_table, slopes):
    n_seq = qrows.shape[0]
    width = N_CHUNK * LANES
    grid_spec = pltpu.PrefetchScalarGridSpec(
        num_scalar_prefetch=1,
        grid=(n_seq,),
        in_specs=[
            pl.BlockSpec(memory_space=pltpu.SMEM),
            pl.BlockSpec((1, ROWS, D_MODEL), lambda s, pt: (s, 0, 0)),
            pl.BlockSpec((DEC_SEQ, 2 * D_MODEL), lambda s, pt: (s, 0)),
        ] + _page_specs(2, D_MODEL),
        out_specs=pl.BlockSpec((DEC_SEQ, D_MODEL), lambda s, pt: (s, 0)),
        scratch_shapes=[pltpu.VMEM((ROWS, width), F32),
                        pltpu.VMEM((H_MOBA * DEC_SEQ, width), F32),
                        pltpu.VMEM((H_MOBA * HEAD_DIM, LANES), F32)],
    )
    return pl.pallas_call(
        _even_decode_kernel,
        out_shape=jax.ShapeDtypeStruct((n_seq * DEC_SEQ, D_MODEL), F32),
        grid_spec=grid_spec,
        compiler_params=_cparams(1),
        name="even_decode",
    )(page_table, slopes, qrows, newkv, *([pool] * N_PAGES))


def _odd_decode_kernel(pt_ref, slopes_ref, qrows_ref, qirows_ref, wi_ref, new_ref, newidx_ref, *refs):
    page_refs = list(refs[:N_PAGES]) + [None]
    idx_refs = refs[N_PAGES:2 * N_PAGES]
    o_ref = refs[2 * N_PAGES]
    key_sc, s_sc = refs[2 * N_PAGES + 1:]
    qrows = qrows_ref[0]
    qirows = qirows_ref[0]
    wi = wi_ref[...]
    newkv = new_ref[...]
    new_k, new_v = newkv[:, 0:D_MODEL], newkv[:, D_MODEL:2 * D_MODEL]
    qi8 = lax.broadcasted_iota(I32, (DEC_SEQ, LANES), 0)
    cl8 = lax.broadcasted_iota(I32, (DEC_SEQ, LANES), 1)

    for c in range(N_CHUNK):
        if c < N_PAGES:
            dots = _dot(qirows, idx_refs[c][0].astype(BF16))
        else:
            dots = _dot_nt(qirows, _pad_new(newidx_ref[...]).astype(BF16))
        dots = jnp.maximum(dots, 0.0)
        score = jnp.zeros((DEC_SEQ, LANES), F32)
        for h in range(IDX_HEADS):
            score = score + wi[:, h:h + 1] * dots[h * DEC_SEQ:(h + 1) * DEC_SEQ, :]
        key = _ordered_key(score)
        if c == N_PAGES:
            key = jnp.where(cl8 <= qi8, key, INT_MIN)
        key_sc[:, c * LANES:(c + 1) * LANES] = key

    def count(pred):
        return jnp.sum(jnp.where(pred(key_sc[...]), 1.0, 0.0), axis=1, keepdims=True)

    thr = _kth_largest_key(lambda cand: count(lambda k: k >= cand), DEC_SEQ, DSA_TOPK)
    need = DSA_TOPK - count(lambda k: k > thr)
    r2 = lax.broadcasted_iota(I32, (LANES, LANES), 0)
    c2 = lax.broadcasted_iota(I32, (LANES, LANES), 1)
    tri = (r2 <= c2).astype(BF16)
    slope = _row_slopes(slopes_ref, N_HEADS, ROWS)
    qi = lax.broadcasted_iota(I32, (ROWS, LANES), 0) % DEC_SEQ
    cl = lax.broadcasted_iota(I32, (ROWS, LANES), 1)

    run = jnp.zeros((DEC_SEQ, 1), F32)
    m = jnp.full((ROWS, 1), NEG_INIT, F32)
    for c in range(N_CHUNK):
        key = key_sc[:, c * LANES:(c + 1) * LANES]
        eq = key == thr
        eqf = jnp.where(eq, 1.0, 0.0)
        prefix = _dot(eqf.astype(BF16), tri) + run
        run = run + jnp.sum(eqf, axis=1, keepdims=True)
        keep8 = ((key > thr) | (eq & (prefix <= need))) & (key != INT_MIN)
        keep = jnp.concatenate([jnp.where(keep8, 1.0, 0.0)] * N_HEADS, axis=0) > 0.0
        if c < N_PAGES:
            dist = (PAST_LEN + qi - (c * PAGE_SIZE + cl)).astype(F32)
        else:
            dist = (qi - cl).astype(F32)
        s = jnp.where(keep, _scores(qrows, page_refs[c], new_k) * SCALE - slope * dist, NEG_MASKED)
        s_sc[:, c * LANES:(c + 1) * LANES] = s
        m = jnp.maximum(m, jnp.max(s, axis=1, keepdims=True))

    l = jnp.zeros((ROWS, 1), F32)
    for c in range(N_CHUNK):
        p = jnp.exp(s_sc[:, c * LANES:(c + 1) * LANES] - m)
        s_sc[:, c * LANES:(c + 1) * LANES] = p
        l = l + jnp.sum(p, axis=1, keepdims=True)
    inv = 1.0 / l
    acc = jnp.zeros((ROWS, D_MODEL), F32)
    for c in range(N_CHUNK):
        p = (s_sc[:, c * LANES:(c + 1) * LANES] * inv).astype(BF16)
        acc = acc + _weighted_values(p, page_refs[c], new_v, 0, D_MODEL)
    o_ref[...] = _diag_heads(acc, N_HEADS)


def _odd_decode(qrows, qirows, wi, newkv, newidx, pool, pool_idx, page_table, slopes):
    n_seq = qrows.shape[0]
    width = N_CHUNK * LANES
    grid_spec = pltpu.PrefetchScalarGridSpec(
        num_scalar_prefetch=1,
        grid=(n_seq,),
        in_specs=[
            pl.BlockSpec(memory_space=pltpu.SMEM),
            pl.BlockSpec((1, ROWS, D_MODEL), lambda s, pt: (s, 0, 0)),
            pl.BlockSpec((1, IDX_HEADS * DEC_SEQ, IDX_DIM), lambda s, pt: (s, 0, 0)),
            pl.BlockSpec((DEC_SEQ, LANES), lambda s, pt: (s, 0)),
            pl.BlockSpec((DEC_SEQ, 2 * D_MODEL), lambda s, pt: (s, 0)),
            pl.BlockSpec((DEC_SEQ, IDX_DIM), lambda s, pt: (s, 0)),
        ] + _page_specs(2, D_MODEL) + _page_specs(IDX_DIM),
        out_specs=pl.BlockSpec((DEC_SEQ, D_MODEL), lambda s, pt: (s, 0)),
        scratch_shapes=[pltpu.VMEM((DEC_SEQ, width), I32),
                        pltpu.VMEM((ROWS, width), F32)],
    )
    return pl.pallas_call(
        _odd_decode_kernel,
        out_shape=jax.ShapeDtypeStruct((n_seq * DEC_SEQ, D_MODEL), F32),
        grid_spec=grid_spec,
        compiler_params=_cparams(1),
        name="odd_decode",
    )(page_table, slopes, qrows, qirows, wi, newkv, newidx, *([pool] * N_PAGES), *([pool_idx] * N_PAGES))


def _layer_norm(x, g, b):
    mu = jnp.mean(x, axis=-1, keepdims=True)
    xc = x - mu
    var = jnp.mean(xc * xc, axis=-1, keepdims=True)
    return xc * lax.rsqrt(var + LN_EPS) * g + b


def _mix_norm_kernel(x_ref, g_ref, b_ref, wr_ref, br_ref, *refs, n_mix):
    o_refs = refs[:n_mix]
    w_refs = refs[n_mix:2 * n_mix]
    y_ref, idx_ref, gate_ref = refs[2 * n_mix:]
    mix = _dot(o_refs[0][...].astype(BF16), w_refs[0][...])
    for o_ref, w_ref in zip(o_refs[1:], w_refs[1:]):
        mix = mix + _dot(o_ref[...].astype(BF16), w_ref[...])
    y = _layer_norm(DEEPNORM_ALPHA * x_ref[...] + mix, g_ref[...], b_ref[...])
    y_ref[...] = y

    logits = _dot(y.astype(BF16), wr_ref[...]) + br_ref[...]
    lane = lax.broadcasted_iota(I32, logits.shape, 1)
    lane_f = lane.astype(F32)
    idx_out = jnp.zeros(logits.shape, I32)
    val_out = jnp.zeros(logits.shape, F32)
    top = None
    den = jnp.zeros((logits.shape[0], 1), F32)
    for k in range(TOP_K):
        mx = jnp.max(logits, axis=1, keepdims=True)
        ix = jnp.min(jnp.where(logits == mx, lane_f, float(LANES)), axis=1, keepdims=True).astype(I32)
        if k == 0:
            top = mx
        e = jnp.exp(mx - top)
        den = den + e
        idx_out = jnp.where(lane == k, ix, idx_out)
        val_out = jnp.where(lane == k, e, val_out)
        logits = jnp.where(lane == ix, -jnp.inf, logits)
    idx_ref[...] = idx_out
    gate_ref[...] = val_out / den


def _mix_norm_route(x, os_, ws, g, b, wr, br, tm=ROW_TILE):
    n_tok = x.shape[0]
    n = len(os_)
    return pl.pallas_call(
        functools.partial(_mix_norm_kernel, n_mix=n),
        out_shape=[jax.ShapeDtypeStruct((n_tok, D_MODEL), F32),
                   jax.ShapeDtypeStruct((n_tok, LANES), I32),
                   jax.ShapeDtypeStruct((n_tok, LANES), F32)],
        grid=(n_tok // tm,),
        in_specs=[pl.BlockSpec((tm, D_MODEL), lambda i: (i, 0)),
                  pl.BlockSpec((1, D_MODEL), lambda i: (0, 0)),
                  pl.BlockSpec((1, D_MODEL), lambda i: (0, 0)),
                  pl.BlockSpec((D_MODEL, LANES), lambda i: (0, 0)),
                  pl.BlockSpec((1, LANES), lambda i: (0, 0))]
        + [pl.BlockSpec((tm, o.shape[1]), lambda i: (i, 0)) for o in os_]
        + [pl.BlockSpec(w.shape, lambda i: (0, 0)) for w in ws],
        out_specs=[pl.BlockSpec((tm, D_MODEL), lambda i: (i, 0)),
                   pl.BlockSpec((tm, LANES), lambda i: (i, 0)),
                   pl.BlockSpec((tm, LANES), lambda i: (i, 0))],
        compiler_params=_cparams(1),
        name="mix_norm_route",
    )(x, g, b, wr, br, *os_, *ws)


def _moe_kernel(be_ref, nb_ref, x_ref, gate_ref, wgu_ref, bgu_ref, wd_ref, bd_ref, y_ref, wgu_sc, wd_sc):
    i = pl.program_id(0)
    used = i < nb_ref[0]
    new_expert = (i == 0) | (be_ref[i] != be_ref[jnp.maximum(i - 1, 0)])

    @pl.when(used & new_expert)
    def _():
        for r in range(0, D_MODEL, CAST_ROWS):
            wgu_sc[r:r + CAST_ROWS, :] = wgu_ref[0, r:r + CAST_ROWS, :].astype(BF16)
        for r in range(0, D_FF, CAST_ROWS):
            wd_sc[r:r + CAST_ROWS, :] = wd_ref[0, r:r + CAST_ROWS, :].astype(BF16)

    @pl.when(used)
    def _():
        h = _dot(x_ref[...].astype(BF16), wgu_sc[...]) + bgu_ref[0]
        glu = jnp.minimum(h[:, :D_FF], SWIGLU_LIMIT)
        lin = jnp.clip(h[:, D_FF:], -SWIGLU_LIMIT, SWIGLU_LIMIT)
        act = glu * jax.nn.sigmoid(SWIGLU_ALPHA * glu) * (lin + 1.0)
        y = _dot(act.astype(BF16), wd_sc[...]) + bd_ref[0]
        y_ref[...] = y * gate_ref[...]

    @pl.when(i >= nb_ref[0])
    def _():
        y_ref[...] = jnp.zeros(y_ref.shape, y_ref.dtype)


def _moe_blocks(x_sorted, slot_gate, block_expert, n_used, wgu, bgu, wd, bd):
    n_slots = x_sorted.shape[0]
    n_blocks = n_slots // MOE_BLOCK
    grid_spec = pltpu.PrefetchScalarGridSpec(
        num_scalar_prefetch=2,
        grid=(n_blocks,),
        in_specs=[
            pl.BlockSpec((MOE_BLOCK, D_MODEL), lambda i, be, nb: (i, 0)),
            pl.BlockSpec((MOE_BLOCK, 1), lambda i, be, nb: (i, 0)),
            pl.BlockSpec((1, D_MODEL, 2 * D_FF), lambda i, be, nb: (be[i], 0, 0)),
            pl.BlockSpec((1, 1, 2 * D_FF), lambda i, be, nb: (be[i], 0, 0)),
            pl.BlockSpec((1, D_FF, D_MODEL), lambda i, be, nb: (be[i], 0, 0)),
            pl.BlockSpec((1, 1, D_MODEL), lambda i, be, nb: (be[i], 0, 0)),
        ],
        out_specs=pl.BlockSpec((MOE_BLOCK, D_MODEL), lambda i, be, nb: (i, 0)),
        scratch_shapes=[pltpu.VMEM((D_MODEL, 2 * D_FF), BF16), pltpu.VMEM((D_FF, D_MODEL), BF16)],
    )
    return pl.pallas_call(
        _moe_kernel,
        out_shape=jax.ShapeDtypeStruct((n_slots, D_MODEL), F32),
        grid_spec=grid_spec,
        compiler_params=_cparams(1),
        name="moe_blocks",
    )(block_expert, n_used, x_sorted, slot_gate, wgu, bgu, wd, bd)


def _ffn_norm_kernel(x_ref, yg_ref, g_ref, b_ref, o_ref):
    ffn = yg_ref[0] + yg_ref[1] + yg_ref[2] + yg_ref[3]
    o_ref[...] = _layer_norm(DEEPNORM_ALPHA * x_ref[...] + ffn, g_ref[...], b_ref[...])


def _ffn_norm(x, yg, g, b, tm=ROW_TILE):
    n_tok = x.shape[0]
    return pl.pallas_call(
        _ffn_norm_kernel,
        out_shape=jax.ShapeDtypeStruct((n_tok, D_MODEL), F32),
        grid=(n_tok // tm,),
        in_specs=[pl.BlockSpec((tm, D_MODEL), lambda i: (i, 0)),
                  pl.BlockSpec((TOP_K, tm, D_MODEL), lambda i: (0, i, 0)),
                  pl.BlockSpec((1, D_MODEL), lambda i: (0, 0)),
                  pl.BlockSpec((1, D_MODEL), lambda i: (0, 0))],
        out_specs=pl.BlockSpec((tm, D_MODEL), lambda i: (i, 0)),
        compiler_params=_cparams(1),
        name="ffn_norm",
    )(x, yg, g, b)


def _moe_layer(x, top_idx, gate, expert0, wgu, bgu, wd, bd, g, b):
    n_tok = x.shape[0]
    n_assign = n_tok * TOP_K
    e_flat = top_idx.reshape(n_assign)
    onehot = (e_flat[:, None] == jnp.arange(N_EXPERTS, dtype=I32)[None, :]).astype(I32)
    csum = jnp.cumsum(onehot, axis=0)
    counts = csum[-1]
    pos = jnp.take_along_axis(csum, e_flat[:, None], axis=1)[:, 0] - 1
    padded = (counts + MOE_BLOCK - 1) // MOE_BLOCK * MOE_BLOCK
    pad_end = jnp.cumsum(padded)
    pad_start = pad_end - padded
    slot = pad_start[e_flat] + pos
    n_blocks = -(-n_assign // MOE_BLOCK) + N_EXPERTS
    n_slots = n_blocks * MOE_BLOCK
    slot_assign = jnp.full((n_slots,), n_assign, I32).at[slot].set(jnp.arange(n_assign, dtype=I32))
    slot_tok = jnp.minimum(slot_assign // TOP_K, n_tok)
    slot_gate = jnp.concatenate([gate.reshape(n_assign), jnp.zeros((1,), F32)])[slot_assign]
    n_used = (pad_end[-1] // MOE_BLOCK).astype(I32)
    blk_start = jnp.arange(n_blocks, dtype=I32) * MOE_BLOCK
    block_expert = jnp.minimum(jnp.searchsorted(pad_end, blk_start, side='right'), N_EXPERTS - 1).astype(I32)
    last_expert = block_expert[jnp.maximum(n_used - 1, 0)]
    block_expert = jnp.where(jnp.arange(n_blocks) < n_used, block_expert, last_expert)
    x_pad = jnp.concatenate([x, jnp.zeros((1, D_MODEL), F32)], axis=0)
    x_sorted = x_pad[slot_tok]
    y_slots = _moe_blocks(x_sorted, slot_gate[:, None], block_expert + expert0, n_used[None], wgu, bgu, wd, bd)
    yg = y_slots[slot.reshape(n_tok, TOP_K).T]
    return _ffn_norm(x, yg, g, b)


def _alibi_slopes(n_heads):
    return jnp.exp2(-8.0 * jnp.arange(1, n_heads + 1, dtype=F32) / n_heads)


def _block_diag_rows(q, n_heads, width):
    n_seq = q.shape[0] // DEC_SEQ
    q4 = q.reshape(n_seq, DEC_SEQ, n_heads, width)
    eye = jnp.eye(n_heads, dtype=q.dtype)
    out = jnp.einsum('sqhd,hg->shqgd', q4, eye)
    return out.reshape(n_seq, n_heads * DEC_SEQ, n_heads * width)


def kernel(x_prompt, x_sample, cache_kv_even, cache_kv_odd, cache_kidx_odd, page_table,
           w_in_even, w_o_even, w_in_odd, w_o_odd, ln_g, ln_b,
           w_router, b_router, w_gate_up, b_gate_up, w_down, b_down):
    batch, seq, _ = x_prompt.shape
    n_seq, dec_seq, _ = x_sample.shape
    n_p = batch * seq
    n_s = n_seq * dec_seq
    hd = N_HEADS * HEAD_DIM
    x = jnp.concatenate([x_prompt.reshape(n_p, D_MODEL), x_sample.reshape(n_s, D_MODEL)], axis=0)

    slopes_even = _alibi_slopes(H_MOBA)
    slopes_odd = _alibi_slopes(N_HEADS)
    wr = jnp.pad(w_router, ((0, 0), (0, 0), (0, LANES - N_EXPERTS))).astype(BF16)
    br = jnp.pad(b_router, ((0, 0), (0, LANES - N_EXPERTS)), constant_values=-jnp.inf)[:, None, :]
    ln_g4 = ln_g[:, :, None, :]
    ln_b4 = ln_b[:, :, None, :]

    def moe(layer, x_mid, top_idx, gate):
        return _moe_layer(x_mid, top_idx[:, :TOP_K], gate[:, :TOP_K], layer * N_EXPERTS,
                          w_gate_up.reshape(-1, D_MODEL, 2 * D_FF), b_gate_up.reshape(-1, 1, 2 * D_FF),
                          w_down.reshape(-1, D_FF, D_MODEL), b_down.reshape(-1, 1, D_MODEL),
                          ln_g4[layer, 1], ln_b4[layer, 1])

    w_in = w_in_even[0].astype(BF16)
    ws = [w_in[:, :hd], w_in[:, hd:]]
    q_p, kv_p = _proj(x, 0, n_p, ws)
    q_s, kv_s = _proj(x, n_p, n_s, ws)
    o_moba = _moba_prompt(q_p, kv_p, slopes_even, batch, seq)
    o_sb = _sb_prompt(q_p, kv_p, batch, seq)
    qrows = _block_diag_rows(q_s.astype(BF16), N_HEADS, HEAD_DIM)
    o_s = _even_decode(qrows, kv_s, _pool_pages_kv(cache_kv_even), page_table, slopes_even)
    o_all = jnp.concatenate([jnp.concatenate([o_moba, o_sb], axis=1), o_s], axis=0)
    w_o = w_o_even[0].astype(BF16)
    x, top_idx, gate = _mix_norm_route(x, [o_all], [w_o], ln_g4[0, 0], ln_b4[0, 0], wr[0], br[0])
    x = moe(0, x, top_idx, gate)
    new_kv_even_p = kv_p.reshape(1, batch, seq, 2, N_HEADS, HEAD_DIM)
    new_kv_even_s = kv_s.reshape(1, n_seq, dec_seq, 2, N_HEADS, HEAD_DIM)

    w_in = w_in_odd[0].astype(BF16)
    n_qkv = 3 * hd
    n_qi = IDX_HEADS * IDX_DIM
    w_kidx = w_in[:, n_qkv + n_qi:n_qkv + n_qi + IDX_DIM]
    w_wi = jnp.pad(w_in[:, n_qkv + n_qi + IDX_DIM:], ((0, 0), (0, LANES - IDX_HEADS)))
    ws = [w_in[:, :hd], w_in[:, hd:n_qkv], w_in[:, n_qkv:n_qkv + n_qi],
          jnp.concatenate([w_kidx, w_kidx], axis=1), w_wi]
    q_p, kv_p, qi_p, kidx2_p, wi_p = _proj(x, 0, n_p, ws)
    q_s, kv_s, qi_s, kidx2_s, wi_s = _proj(x, n_p, n_s, ws)
    mask = _dsa_select_prompt(qi_p, wi_p, kidx2_p, batch, seq)
    o_p = _dsa_attn_prompt(q_p, kv_p, mask, slopes_odd, batch, seq)
    qrows = _block_diag_rows(q_s.astype(BF16), N_HEADS, HEAD_DIM)
    qirows = qi_s.astype(BF16).reshape(n_seq, DEC_SEQ, IDX_HEADS, IDX_DIM).transpose(0, 2, 1, 3)
    qirows = qirows.reshape(n_seq, IDX_HEADS * DEC_SEQ, IDX_DIM)
    kidx_s = kidx2_s[:, :IDX_DIM]
    pool_idx = jnp.transpose(cache_kidx_odd[0], (0, 2, 1))
    o_s = _odd_decode(qrows, qirows, wi_s, kv_s, kidx_s, _pool_pages_kv(cache_kv_odd), pool_idx,
                      page_table, slopes_odd)
    o_all = jnp.concatenate([o_p, o_s], axis=0)
    w_o = w_o_odd[0].astype(BF16)
    x, top_idx, gate = _mix_norm_route(x, [o_all], [w_o], ln_g4[1, 0], ln_b4[1, 0], wr[1], br[1])
    x = moe(1, x, top_idx, gate)
    new_kv_odd_p = kv_p.reshape(1, batch, seq, 2, N_HEADS, HEAD_DIM)
    new_kv_odd_s = kv_s.reshape(1, n_seq, dec_seq, 2, N_HEADS, HEAD_DIM)
    new_kidx_p = kidx2_p[:, :IDX_DIM].reshape(1, batch, seq, IDX_DIM)
    new_kidx_s = kidx_s.reshape(1, n_seq, dec_seq, IDX_DIM)

    y_prompt = x[:n_p].reshape(batch, seq, D_MODEL)
    y_sample = x[n_p:].reshape(n_seq, dec_seq, D_MODEL)
    return (y_prompt, y_sample, new_kv_even_p, new_kv_even_s, new_kv_odd_p, new_kv_odd_s,
            new_kidx_p, new_kidx_s)
```

```python
import functools

import jax
import jax.numpy as jnp
from jax import lax
from jax.experimental import pallas as pl
from jax.experimental.pallas import tpu as pltpu

F32 = jnp.float32
BF16 = jnp.bfloat16
I32 = jnp.int32

D_MODEL = 1024
HEAD_DIM = 64
N_HEADS = 16
H_MOBA = 8
MOBA_BLOCK = 256
MOBA_TOPK = 3
DSA_TOPK = 256
IDX_HEADS = 8
IDX_DIM = 64
N_EXPERTS = 32
TOP_K = 4
D_FF = 1024
SWIGLU_LIMIT = 7.0
SWIGLU_ALPHA = 1.702
LN_EPS = 1e-5
DEPTH = 2
DEEPNORM_ALPHA = (2 * DEPTH) ** 0.25
PAGE_SIZE = 128
PAST_LEN = 2048
N_PAGES = PAST_LEN // PAGE_SIZE
DEC_SEQ = 8

LANES = 128
TQ = 256
PAIRS = 2
MOE_BLOCK = 256
CAST_ROWS = 128
ROW_TILE = 512
PROMPT_PROJ_TILE = 256
NEG_MASKED = -2e30
NEG_INIT = -1e30
INT_MIN = -2 ** 31
SB_DEAD_TAIL = -104.0
VMEM_LIMIT = 56 * 1024 * 1024
SCALE = HEAD_DIM ** -0.5


def _cparams(n_grid):
    return pltpu.CompilerParams(dimension_semantics=("arbitrary",) * n_grid,
                                vmem_limit_bytes=VMEM_LIMIT)


def _dot(a, b):
    return jnp.dot(a, b, preferred_element_type=F32)


def _dot_nt(a, b):
    return lax.dot_general(a, b, (((1,), (1,)), ((), ())), preferred_element_type=F32)


def _split_dot(x, tri):
    hi = x.astype(BF16)
    lo = (x - hi.astype(F32)).astype(BF16)
    return _dot(hi, tri) + _dot(lo, tri)


def _log_sigmoid_neg(z):
    return -(jnp.maximum(z, 0.0) + jnp.log(1.0 + jnp.exp(-jnp.abs(z))))


def _proj_kernel(x_ref, *refs, n_out, n_out_t):
    n_w = n_out + n_out_t
    xb = x_ref[...].astype(BF16)
    for w_ref, o_ref in zip(refs[:n_out], refs[n_w:n_w + n_out]):
        o_ref[...] = _dot(xb, w_ref[...])
    for wt_ref, ot_ref in zip(refs[n_out:n_w], refs[n_w + n_out:]):
        ot_ref[0] = _dot_nt(wt_ref[...], xb)


def _proj(x, row0, n_rows, ws, wts=(), seq=None, tm=ROW_TILE):
    assert row0 % tm == 0 and n_rows % tm == 0
    blk0 = row0 // tm
    out_shape = [jax.ShapeDtypeStruct((n_rows, w.shape[1]), F32) for w in ws]
    out_specs = [pl.BlockSpec((tm, w.shape[1]), lambda i: (i, 0)) for w in ws]
    if wts:
        per_seq = seq // tm
        out_shape += [jax.ShapeDtypeStruct((n_rows // seq, wt.shape[0], seq), F32) for wt in wts]
        out_specs += [pl.BlockSpec((1, wt.shape[0], tm), lambda i: (i // per_seq, 0, i % per_seq)) for wt in wts]
    return pl.pallas_call(
        functools.partial(_proj_kernel, n_out=len(ws), n_out_t=len(wts)),
        out_shape=out_shape,
        grid=(n_rows // tm,),
        in_specs=[pl.BlockSpec((tm, D_MODEL), lambda i: (i + blk0, 0))]
        + [pl.BlockSpec(w.shape, lambda i: (0, 0)) for w in list(ws) + list(wts)],
        out_specs=out_specs,
        compiler_params=_cparams(1),
        name="proj",
    )(x, *ws, *wts)


def _kv_from_feature_major(kv_t, seq):
    batch = kv_t.shape[0]
    return jnp.transpose(kv_t.reshape(batch, 2, N_HEADS, HEAD_DIM, seq), (0, 4, 1, 2, 3))[None]


def _head_queries(q, scale):
    lane = lax.broadcasted_iota(I32, q.shape, 1)
    qs = q * scale
    return [jnp.where((lane >= hh * HEAD_DIM) & (lane < (hh + 1) * HEAD_DIM), qs, 0.0).astype(BF16)
            for hh in range(2)]


def _softmax_tile(s, kv_b, state):
    m, l, acc = state
    m_new = jnp.maximum(m, jnp.max(s, axis=1, keepdims=True))
    p = jnp.exp(s - m_new)
    alpha = jnp.exp(m - m_new)
    l = alpha * l + jnp.sum(p, axis=1, keepdims=True)
    acc = alpha * acc + _dot(p.astype(BF16), kv_b)
    return m_new, l, acc


def _init_state():
    return (jnp.full((TQ, 1), NEG_INIT, F32), jnp.zeros((TQ, 1), F32), jnp.zeros((TQ, LANES), F32))


def _finish_pair(states):
    lane = lax.broadcasted_iota(I32, (TQ, LANES), 1)
    o0 = states[0][2] / states[0][1]
    o1 = states[1][2] / states[1][1]
    return jnp.where(lane < HEAD_DIM, o0, o1)


def _moba_kernel(slopes_ref, q_ref, k_ref, v_ref, o_ref, kmean_sc, *, nt):
    g = pl.program_id(1)
    i = pl.program_id(2)

    @pl.when(i == 0)
    def _():
        for n in range(nt):
            kmean_sc[n:n + 1, :] = jnp.sum(k_ref[n * TQ:(n + 1) * TQ, :], axis=0, keepdims=True) * (1.0 / MOBA_BLOCK)

    blk = lax.broadcasted_iota(I32, (TQ, nt), 1)
    past = blk < i
    row = lax.broadcasted_iota(I32, (TQ, TQ), 0)
    col = lax.broadcasted_iota(I32, (TQ, TQ), 1)
    rel = (row - col).astype(F32)

    q_att, sels, slopes = [], [], []
    for pr in range(PAIRS):
        q = q_ref[:, pr * LANES:(pr + 1) * LANES]
        km = kmean_sc[:, pr * LANES:(pr + 1) * LANES].astype(BF16)
        q_att.extend(_head_queries(q, SCALE))
        for hh, q_gate in enumerate(_head_queries(q, 1.0)):
            gate = jnp.where(past, _dot_nt(q_gate, km), -jnp.inf)
            rank = jnp.zeros((TQ, nt), I32)
            for n2 in range(nt):
                gn = gate[:, n2:n2 + 1]
                rank = rank + ((gn > gate) | ((gn == gate) & (n2 < blk))).astype(I32)
            sels.append((past & (rank < MOBA_TOPK)).astype(F32))
            slopes.append(slopes_ref[(g * PAIRS + pr) * 2 + hh])

    def kv_tile(off, pr):
        return (k_ref[pl.ds(off, TQ), pr * LANES:(pr + 1) * LANES].astype(BF16),
                v_ref[pl.ds(off, TQ), pr * LANES:(pr + 1) * LANES].astype(BF16))

    start = pl.multiple_of(i * TQ, TQ)
    init = []
    for pr in range(PAIRS):
        kb, vb = kv_tile(start, pr)
        for hh in range(2):
            s = _dot_nt(q_att[2 * pr + hh], kb) - slopes[2 * pr + hh] * rel
            s = jnp.where(col <= row, s, NEG_MASKED)
            init.extend(_softmax_tile(s, vb, _init_state()))

    def body(j, carry):
        off = pl.multiple_of(j * TQ, TQ)
        dist = rel + ((i - j) * TQ).astype(F32)
        out = []
        for pr in range(PAIRS):
            kb, vb = kv_tile(off, pr)
            for hh in range(2):
                h = 2 * pr + hh
                sel_j = jnp.sum(jnp.where(blk == j, sels[h], 0.0), axis=1, keepdims=True) > 0.0
                s = _dot_nt(q_att[h], kb) - slopes[h] * dist
                s = jnp.where(sel_j, s, NEG_MASKED)
                out.extend(_softmax_tile(s, vb, carry[3 * h:3 * h + 3]))
        return tuple(out)

    carry = lax.fori_loop(0, i, body, tuple(init))
    for pr in range(PAIRS):
        o_ref[:, pr * LANES:(pr + 1) * LANES] = _finish_pair([carry[6 * pr:6 * pr + 3], carry[6 * pr + 3:6 * pr + 6]])


def _moba_prompt(q, kv, slopes, batch, seq):
    nt = seq // TQ
    width = PAIRS * LANES
    v_blk0 = N_HEADS * HEAD_DIM // width
    return pl.pallas_call(
        functools.partial(_moba_kernel, nt=nt),
        out_shape=jax.ShapeDtypeStruct((batch * seq, H_MOBA * HEAD_DIM), F32),
        grid=(batch, H_MOBA // (2 * PAIRS), nt),
        in_specs=[
            pl.BlockSpec(memory_space=pltpu.SMEM),
            pl.BlockSpec((TQ, width), lambda b, g, i: (b * nt + i, g)),
            pl.BlockSpec((seq, width), lambda b, g, i: (b, g)),
            pl.BlockSpec((seq, width), lambda b, g, i: (b, v_blk0 + g)),
        ],
        out_specs=pl.BlockSpec((TQ, width), lambda b, g, i: (b * nt + i, g)),
        scratch_shapes=[pltpu.VMEM((nt, width), F32)],
        compiler_params=_cparams(3),
        name="moba_prompt",
    )(slopes, q, kv, kv)


def _sb_tile(qh, kb, vb, tri, carry, acc, strict_mask):
    z = _dot_nt(qh, kb)
    lk = _log_sigmoid_neg(z)
    if strict_mask is not None:
        lk = jnp.where(strict_mask, lk, 0.0)
    incl = _split_dot(lk, tri)
    w = jnp.exp(z + incl + carry)
    if strict_mask is not None:
        w = jnp.where(strict_mask, w, 0.0)
    return carry + incl[:, 0:1], acc + _dot(w.astype(BF16), vb)


def _sb_kernel(q_ref, k_ref, v_ref, o_ref):
    i = pl.program_id(2)
    q_att = _head_queries(q_ref[...], SCALE)
    row = lax.broadcasted_iota(I32, (TQ, TQ), 0)
    col = lax.broadcasted_iota(I32, (TQ, TQ), 1)
    tri = (row >= col).astype(BF16)
    strict = col < row

    start = pl.multiple_of(i * TQ, TQ)
    kb = k_ref[pl.ds(start, TQ), :].astype(BF16)
    vb = v_ref[pl.ds(start, TQ), :].astype(BF16)
    init = []
    for hh in range(2):
        init.extend(_sb_tile(q_att[hh], kb, vb, tri, jnp.zeros((TQ, 1), F32),
                             jnp.zeros((TQ, LANES), F32), strict))

    def live(tail0, tail1):
        return jnp.max(jnp.maximum(tail0, tail1)) > SB_DEAD_TAIL

    def cond(state):
        return (state[0] < i) & state[1]

    def body(state):
        t, carry = state[0], state[2:]
        j = i - 1 - t
        off = pl.multiple_of(j * TQ, TQ)
        kb = k_ref[pl.ds(off, TQ), :].astype(BF16)
        vb = v_ref[pl.ds(off, TQ), :].astype(BF16)
        out = []
        for hh in range(2):
            out.extend(_sb_tile(q_att[hh], kb, vb, tri, carry[2 * hh], carry[2 * hh + 1], None))
        return (t + 1, live(out[0], out[2])) + tuple(out)

    state = lax.while_loop(cond, body, (jnp.int32(0), live(init[0], init[2])) + tuple(init))
    lane = lax.broadcasted_iota(I32, (TQ, LANES), 1)
    o_ref[...] = jnp.where(lane < HEAD_DIM, state[3], state[5])


def _sb_prompt(q, kv, batch, seq):
    nt = seq // TQ
    n_sb = N_HEADS - H_MOBA
    hp0 = H_MOBA // 2
    return pl.pallas_call(
        _sb_kernel,
        out_shape=jax.ShapeDtypeStruct((batch * seq, n_sb * HEAD_DIM), F32),
        grid=(batch, n_sb // 2, nt),
        in_specs=[
            pl.BlockSpec((TQ, LANES), lambda b, hp, i: (b * nt + i, hp0 + hp)),
            pl.BlockSpec((seq, LANES), lambda b, hp, i: (b, hp0 + hp)),
            pl.BlockSpec((seq, LANES), lambda b, hp, i: (b, N_HEADS // 2 + hp0 + hp)),
        ],
        out_specs=pl.BlockSpec((TQ, LANES), lambda b, hp, i: (b * nt + i, hp)),
        compiler_params=_cparams(3),
        name="sb_prompt",
    )(q, kv, kv)


def _ordered_key(score):
    bits = lax.bitcast_convert_type(score + 0.0, I32)
    return jnp.where(bits < 0, bits ^ 0x7FFFFFFF, bits)


def _kth_largest_key(count_ge, rows, k):
    t = jnp.where(count_ge(jnp.zeros((rows, 1), I32)) >= k, 0, INT_MIN).astype(I32)

    def body(b, t):
        cand = t | (1 << (30 - b))
        return jnp.where(count_ge(cand) >= k, cand, t)

    return lax.fori_loop(0, 31, body, t)


def _dsa_select_kernel(qi_ref, wi_ref, kidx_ref, mask_ref, key_sc, *, nt):
    i = pl.program_id(1)
    qi = qi_ref[...]
    wi = wi_ref[...]
    row = lax.broadcasted_iota(I32, (TQ, TQ), 0)
    col = lax.broadcasted_iota(I32, (TQ, TQ), 1)
    qh = []
    for p in range(IDX_HEADS // 2):
        qh.extend(_head_queries(qi[:, p * LANES:(p + 1) * LANES], 1.0))
    wcol = [wi[:, h:h + 1] for h in range(IDX_HEADS)]

    def score_tile(j):
        off = pl.multiple_of(j * TQ, TQ)
        kb = kidx_ref[pl.ds(off, TQ), :].astype(BF16)
        score = jnp.zeros((TQ, TQ), F32)
        for h in range(IDX_HEADS):
            score = score + wcol[h] * jnp.maximum(_dot_nt(qh[h], kb), 0.0)
        return _ordered_key(score)

    def fill(j, c):
        off = pl.multiple_of(j * TQ, TQ)
        key_sc[:, pl.ds(off, TQ)] = score_tile(j)
        return c

    lax.fori_loop(0, i, fill, 0)
    start = pl.multiple_of(i * TQ, TQ)
    key_sc[:, pl.ds(start, TQ)] = jnp.where(col <= row, score_tile(i), INT_MIN)

    def count(pred):
        def body(j, acc):
            off = pl.multiple_of(j * TQ, TQ)
            hit = jnp.where(pred(key_sc[:, pl.ds(off, TQ)]), 1.0, 0.0)
            return acc + hit[:, :LANES] + hit[:, LANES:]
        acc = lax.fori_loop(0, i + 1, body, jnp.zeros((TQ, LANES), F32))
        return jnp.sum(acc, axis=1, keepdims=True)

    thr = _kth_largest_key(lambda cand: count(lambda k: k >= cand), TQ, DSA_TOPK)
    need = DSA_TOPK - count(lambda k: k > thr)
    tri = (row <= col).astype(BF16)

    mask_ref[...] = jnp.zeros(mask_ref.shape, mask_ref.dtype)

    def emit(j, run):
        off = pl.multiple_of(j * TQ, TQ)
        key = key_sc[:, pl.ds(off, TQ)]
        eq = key == thr
        eqf = jnp.where(eq, 1.0, 0.0)
        prefix = _dot(eqf.astype(BF16), tri) + run
        keep = ((key > thr) | (eq & (prefix <= need))) & (key != INT_MIN)
        mask_ref[:, pl.ds(off, TQ)] = jnp.where(keep, 1.0, 0.0).astype(mask_ref.dtype)
        return run + jnp.sum(eqf, axis=1, keepdims=True)

    lax.fori_loop(0, i + 1, emit, jnp.zeros((TQ, 1), F32))


def _dsa_select_prompt(qi, wi, kidx2, batch, seq):
    nt = seq // TQ
    return pl.pallas_call(
        functools.partial(_dsa_select_kernel, nt=nt),
        out_shape=jax.ShapeDtypeStruct((batch * seq, seq), BF16),
        grid=(batch, nt),
        in_specs=[
            pl.BlockSpec((TQ, IDX_HEADS * IDX_DIM), lambda b, i: (b * nt + i, 0)),
            pl.BlockSpec((TQ, LANES), lambda b, i: (b * nt + i, 0)),
            pl.BlockSpec((seq, LANES), lambda b, i: (b, 0)),
        ],
        out_specs=pl.BlockSpec((TQ, seq), lambda b, i: (b * nt + i, 0)),
        scratch_shapes=[pltpu.VMEM((TQ, seq), I32)],
        compiler_params=_cparams(2),
        name="dsa_select_prompt",
    )(qi, wi, kidx2)


def _dsa_attn_kernel(slopes_ref, q_ref, k_ref, v_ref, mask_ref, o_ref):
    g = pl.program_id(1)
    i = pl.program_id(2)
    row = lax.broadcasted_iota(I32, (TQ, TQ), 0)
    col = lax.broadcasted_iota(I32, (TQ, TQ), 1)
    rel = (row - col).astype(F32)
    q_att, slopes = [], []
    for pr in range(PAIRS):
        q_att.extend(_head_queries(q_ref[:, pr * LANES:(pr + 1) * LANES], SCALE))
        slopes.extend(slopes_ref[(g * PAIRS + pr) * 2 + hh] for hh in range(2))

    def body(j, carry):
        off = pl.multiple_of(j * TQ, TQ)
        keep = mask_ref[:, pl.ds(off, TQ)] > 0
        dist = rel + ((i - j) * TQ).astype(F32)
        out = []
        for pr in range(PAIRS):
            kb = k_ref[pl.ds(off, TQ), pr * LANES:(pr + 1) * LANES].astype(BF16)
            vb = v_ref[pl.ds(off, TQ), pr * LANES:(pr + 1) * LANES].astype(BF16)
            for hh in range(2):
                h = 2 * pr + hh
                s = _dot_nt(q_att[h], kb) - slopes[h] * dist
                s = jnp.where(keep, s, NEG_MASKED)
                out.extend(_softmax_tile(s, vb, carry[3 * h:3 * h + 3]))
        return tuple(out)

    carry = lax.fori_loop(0, i + 1, body, _init_state() * (2 * PAIRS))
    for pr in range(PAIRS):
        o_ref[:, pr * LANES:(pr + 1) * LANES] = _finish_pair([carry[6 * pr:6 * pr + 3], carry[6 * pr + 3:6 * pr + 6]])


def _dsa_attn_prompt(q, kv, mask, slopes, batch, seq):
    nt = seq // TQ
    width = PAIRS * LANES
    v_blk0 = N_HEADS * HEAD_DIM // width
    return pl.pallas_call(
        _dsa_attn_kernel,
        out_shape=jax.ShapeDtypeStruct((batch * seq, N_HEADS * HEAD_DIM), F32),
        grid=(batch, N_HEADS // (2 * PAIRS), nt),
        in_specs=[
            pl.BlockSpec(memory_space=pltpu.SMEM),
            pl.BlockSpec((TQ, width), lambda b, g, i: (b * nt + i, g)),
            pl.BlockSpec((seq, width), lambda b, g, i: (b, g)),
            pl.BlockSpec((seq, width), lambda b, g, i: (b, v_blk0 + g)),
            pl.BlockSpec((TQ, seq), lambda b, g, i: (b * nt + i, 0)),
        ],
        out_specs=pl.BlockSpec((TQ, width), lambda b, g, i: (b * nt + i, g)),
        compiler_params=_cparams(3),
        name="dsa_attn_prompt",
    )(slopes, q, kv, kv, mask)


N_CHUNK = N_PAGES + 1
ROWS = N_HEADS * DEC_SEQ


def _row_slopes(slopes_ref, n_heads, rows):
    r = lax.broadcasted_iota(I32, (rows, 1), 0) // DEC_SEQ
    out = jnp.zeros((rows, 1), F32)
    for h in range(n_heads):
        out = jnp.where(r == h, slopes_ref[h], out)
    return out


def _pad_new(x8):
    return jnp.concatenate([x8, jnp.zeros((PAGE_SIZE - DEC_SEQ, x8.shape[1]), x8.dtype)], axis=0)


def _diag_heads(acc, n_heads):
    lane = lax.broadcasted_iota(I32, (DEC_SEQ, acc.shape[1]), 1) // HEAD_DIM
    out = jnp.zeros((DEC_SEQ, acc.shape[1]), F32)
    for h in range(n_heads):
        out = jnp.where(lane == h, acc[h * DEC_SEQ:(h + 1) * DEC_SEQ, :], out)
    return out


def _scores(qrows, page_ref, new_rows):
    if page_ref is not None:
        return _dot(qrows, page_ref[0, 0].astype(BF16))
    return _dot_nt(qrows, _pad_new(new_rows).astype(BF16))


def _weighted_values(p, page_ref, new_rows, lo, n):
    if page_ref is not None:
        return _dot_nt(p, page_ref[0, 1, lo:lo + n, :].astype(BF16))
    return _dot(p, _pad_new(new_rows[:, lo:lo + n]).astype(BF16))


def _even_decode_kernel(pt_ref, slopes_ref, qrows_ref, new_ref, *refs):
    page_refs = list(refs[:N_PAGES]) + [None]
    o_ref = refs[N_PAGES]
    s_sc, p_sc, km_sc = refs[N_PAGES + 1:]
    half = H_MOBA * HEAD_DIM
    mrows = H_MOBA * DEC_SEQ
    qrows = qrows_ref[0]
    newkv = new_ref[...]
    new_k, new_v = newkv[:, 0:D_MODEL], newkv[:, D_MODEL:2 * D_MODEL]

    km_sc[...] = jnp.zeros(km_sc.shape, F32)
    for c in range(N_CHUNK):
        s_sc[:, c * LANES:(c + 1) * LANES] = _scores(qrows, page_refs[c], new_k) * SCALE
        if c < N_PAGES:
            part = jnp.sum(page_refs[c][0, 0, 0:half, :], axis=1, keepdims=True)
            n = c // 2
            if c % 2 == 0:
                km_sc[:, n:n + 1] = part
            else:
                km_sc[:, n:n + 1] = (km_sc[:, n:n + 1] + part) * (1.0 / MOBA_BLOCK)

    nb = N_PAGES // 2
    nb_pad = km_sc.shape[1]
    blk = lax.broadcasted_iota(I32, (mrows, nb_pad), 1)
    gate = jnp.where(blk < nb, _dot(qrows[0:mrows, 0:half], km_sc[...].astype(BF16)), -jnp.inf)
    rank = jnp.zeros((mrows, nb_pad), I32)
    for n2 in range(nb):
        gn = gate[:, n2:n2 + 1]
        rank = rank + ((gn > gate) | ((gn == gate) & (n2 < blk))).astype(I32)
    sel = jnp.where((blk < nb) & (rank < MOBA_TOPK), 1.0, 0.0)
    slope = _row_slopes(slopes_ref, H_MOBA, mrows)
    qi = lax.broadcasted_iota(I32, (mrows, LANES), 0) % DEC_SEQ
    cl = lax.broadcasted_iota(I32, (mrows, LANES), 1)

    def moba_scores(c):
        s = s_sc[0:mrows, c * LANES:(c + 1) * LANES]
        if c < N_PAGES:
            dist = (PAST_LEN + qi - (c * PAGE_SIZE + cl)).astype(F32)
            keep = jnp.broadcast_to(sel[:, c // 2:c // 2 + 1], (mrows, LANES)) > 0.5
        else:
            dist = (qi - cl).astype(F32)
            keep = cl <= qi
        return jnp.where(keep, s - slope * dist, NEG_MASKED)

    m = jnp.full((mrows, 1), NEG_INIT, F32)
    for c in range(N_CHUNK):
        sc = moba_scores(c)
        p_sc[:, c * LANES:(c + 1) * LANES] = sc
        m = jnp.maximum(m, jnp.max(sc, axis=1, keepdims=True))
    l = jnp.zeros((mrows, 1), F32)
    for c in range(N_CHUNK):
        p = jnp.exp(p_sc[:, c * LANES:(c + 1) * LANES] - m)
        p_sc[:, c * LANES:(c + 1) * LANES] = p
        l = l + jnp.sum(p, axis=1, keepdims=True)
    inv = 1.0 / l
    acc = jnp.zeros((mrows, half), F32)
    for c in range(N_CHUNK):
        p = (p_sc[:, c * LANES:(c + 1) * LANES] * inv).astype(BF16)
        acc = acc + _weighted_values(p, page_refs[c], new_v, 0, half)
    o_ref[:, 0:half] = _diag_heads(acc, H_MOBA)

    r2 = lax.broadcasted_iota(I32, (LANES, LANES), 0)
    c2 = lax.broadcasted_iota(I32, (LANES, LANES), 1)
    tri = (r2 >= c2).astype(BF16)
    carry = jnp.zeros((mrows, 1), F32)
    acc = jnp.zeros((mrows, half), F32)
    for c in range(N_CHUNK - 1, -1, -1):
        z = s_sc[mrows:2 * mrows, c * LANES:(c + 1) * LANES]
        lk = _log_sigmoid_neg(z)
        if c == N_PAGES:
            strict = cl < qi
            lk = jnp.where(strict, lk, 0.0)
        incl = _split_dot(lk, tri)
        w = jnp.exp(z + incl + carry)
        if c == N_PAGES:
            w = jnp.where(strict, w, 0.0)
        carry = carry + incl[:, 0:1]
        acc = acc + _weighted_values(w.astype(BF16), page_refs[c], new_v, half, half)
    o_ref[:, half:2 * half] = _diag_heads(acc, N_HEADS - H_MOBA)


def _page_specs(*tail):
    def spec(p):
        return pl.BlockSpec((1,) + tail + (PAGE_SIZE,), lambda s, pt: (pt[s, p],) + (0,) * (len(tail) + 1))
    return [spec(p) for p in range(N_PAGES)]


def _pool_pages_kv(cache):
    pool = jnp.transpose(cache[0], (0, 2, 3, 4, 1))
    return pool.reshape(pool.shape[0], 2, D_MODEL, PAGE_SIZE)


def _even_decode(qrows, newkv, pool, page_table, slopes):
    n_seq = qrows.shape[0]
    width = N_CHUNK * LANES
    grid_spec = pltpu.PrefetchScalarGridSpec(
        num_scalar_prefetch=1,
        grid=(n_seq,),
        in_specs=[
            pl.BlockSpec(memory_space=pltpu.SMEM),
            pl.BlockSpec((1, ROWS, D_MODEL), lambda s, pt: (s, 0, 0)),
            pl.BlockSpec((DEC_SEQ, 2 * D_MODEL), lambda s, pt: (s, 0)),
        ] + _page_specs(2, D_MODEL),
        out_specs=pl.BlockSpec((DEC_SEQ, D_MODEL), lambda s, pt: (s, 0)),
        scratch_shapes=[pltpu.VMEM((ROWS, width), F32),
                        pltpu.VMEM((H_MOBA * DEC_SEQ, width), F32),
                        pltpu.VMEM((H_MOBA * HEAD_DIM, LANES), F32)],
    )
    return pl.pallas_call(
        _even_decode_kernel,
        out_shape=jax.ShapeDtypeStruct((n_seq * DEC_SEQ, D_MODEL), F32),
        grid_spec=grid_spec,
        compiler_params=_cparams(1),
        name="even_decode",
    )(page_table, slopes, qrows, newkv, *([pool] * N_PAGES))


def _odd_decode_kernel(pt_ref, slopes_ref, qrows_ref, qirows_ref, wi_ref, new_ref, newidx_ref, *refs):
    page_refs = list(refs[:N_PAGES]) + [None]
    idx_refs = refs[N_PAGES:2 * N_PAGES]
    o_ref = refs[2 * N_PAGES]
    key_sc, s_sc = refs[2 * N_PAGES + 1:]
    qrows = qrows_ref[0]
    qirows = qirows_ref[0]
    wi = wi_ref[...]
    newkv = new_ref[...]
    new_k, new_v = newkv[:, 0:D_MODEL], newkv[:, D_MODEL:2 * D_MODEL]
    qi8 = lax.broadcasted_iota(I32, (DEC_SEQ, LANES), 0)
    cl8 = lax.broadcasted_iota(I32, (DEC_SEQ, LANES), 1)

    for c in range(N_CHUNK):
        if c < N_PAGES:
            dots = _dot(qirows, idx_refs[c][0].astype(BF16))
        else:
            dots = _dot_nt(qirows, _pad_new(newidx_ref[...]).astype(BF16))
        dots = jnp.maximum(dots, 0.0)
        score = jnp.zeros((DEC_SEQ, LANES), F32)
        for h in range(IDX_HEADS):
            score = score + wi[:, h:h + 1] * dots[h * DEC_SEQ:(h + 1) * DEC_SEQ, :]
        key = _ordered_key(score)
        if c == N_PAGES:
            key = jnp.where(cl8 <= qi8, key, INT_MIN)
        key_sc[:, c * LANES:(c + 1) * LANES] = key

    def count(pred):
        return jnp.sum(jnp.where(pred(key_sc[...]), 1.0, 0.0), axis=1, keepdims=True)

    thr = _kth_largest_key(lambda cand: count(lambda k: k >= cand), DEC_SEQ, DSA_TOPK)
    need = DSA_TOPK - count(lambda k: k > thr)
    r2 = lax.broadcasted_iota(I32, (LANES, LANES), 0)
    c2 = lax.broadcasted_iota(I32, (LANES, LANES), 1)
    tri = (r2 <= c2).astype(BF16)
    slope = _row_slopes(slopes_ref, N_HEADS, ROWS)
    qi = lax.broadcasted_iota(I32, (ROWS, LANES), 0) % DEC_SEQ
    cl = lax.broadcasted_iota(I32, (ROWS, LANES), 1)

    run = jnp.zeros((DEC_SEQ, 1), F32)
    m = jnp.full((ROWS, 1), NEG_INIT, F32)
    for c in range(N_CHUNK):
        key = key_sc[:, c * LANES:(c + 1) * LANES]
        eq = key == thr
        eqf = jnp.where(eq, 1.0, 0.0)
        prefix = _dot(eqf.astype(BF16), tri) + run
        run = run + jnp.sum(eqf, axis=1, keepdims=True)
        keep8 = ((key > thr) | (eq & (prefix <= need))) & (key != INT_MIN)
        keep = jnp.concatenate([jnp.where(keep8, 1.0, 0.0)] * N_HEADS, axis=0) > 0.0
        if c < N_PAGES:
            dist = (PAST_LEN + qi - (c * PAGE_SIZE + cl)).astype(F32)
        else:
            dist = (qi - cl).astype(F32)
        s = jnp.where(keep, _scores(qrows, page_refs[c], new_k) * SCALE - slope * dist, NEG_MASKED)
        s_sc[:, c * LANES:(c + 1) * LANES] = s
        m = jnp.maximum(m, jnp.max(s, axis=1, keepdims=True))

    l = jnp.zeros((ROWS, 1), F32)
    for c in range(N_CHUNK):
        p = jnp.exp(s_sc[:, c * LANES:(c + 1) * LANES] - m)
        s_sc[:, c * LANES:(c + 1) * LANES] = p
        l = l + jnp.sum(p, axis=1, keepdims=True)
    inv = 1.0 / l
    acc = jnp.zeros((ROWS, D_MODEL), F32)
    for c in range(N_CHUNK):
        p = (s_sc[:, c * LANES:(c + 1) * LANES] * inv).astype(BF16)
        acc = acc + _weighted_values(p, page_refs[c], new_v, 0, D_MODEL)
    o_ref[...] = _diag_heads(acc, N_HEADS)


def _odd_decode(qrows, qirows, wi, newkv, newidx, pool, pool_idx, page_table, slopes):
    n_seq = qrows.shape[0]
    width = N_CHUNK * LANES
    grid_spec = pltpu.PrefetchScalarGridSpec(
        num_scalar_prefetch=1,
        grid=(n_seq,),
        in_specs=[
            pl.BlockSpec(memory_space=pltpu.SMEM),
            pl.BlockSpec((1, ROWS, D_MODEL), lambda s, pt: (s, 0, 0)),
            pl.BlockSpec((1, IDX_HEADS * DEC_SEQ, IDX_DIM), lambda s, pt: (s, 0, 0)),
            pl.BlockSpec((DEC_SEQ, LANES), lambda s, pt: (s, 0)),
            pl.BlockSpec((DEC_SEQ, 2 * D_MODEL), lambda s, pt: (s, 0)),
            pl.BlockSpec((DEC_SEQ, IDX_DIM), lambda s, pt: (s, 0)),
        ] + _page_specs(2, D_MODEL) + _page_specs(IDX_DIM),
        out_specs=pl.BlockSpec((DEC_SEQ, D_MODEL), lambda s, pt: (s, 0)),
        scratch_shapes=[pltpu.VMEM((DEC_SEQ, width), I32),
                        pltpu.VMEM((ROWS, width), F32)],
    )
    return pl.pallas_call(
        _odd_decode_kernel,
        out_shape=jax.ShapeDtypeStruct((n_seq * DEC_SEQ, D_MODEL), F32),
        grid_spec=grid_spec,
        compiler_params=_cparams(1),
        name="odd_decode",
    )(page_table, slopes, qrows, qirows, wi, newkv, newidx, *([pool] * N_PAGES), *([pool_idx] * N_PAGES))


def _layer_norm(x, g, b):
    mu = jnp.mean(x, axis=-1, keepdims=True)
    xc = x - mu
    var = jnp.mean(xc * xc, axis=-1, keepdims=True)
    return xc * lax.rsqrt(var + LN_EPS) * g + b


def _mix_norm_kernel(x_ref, g_ref, b_ref, wr_ref, br_ref, *refs, n_mix):
    o_refs = refs[:n_mix]
    w_refs = refs[n_mix:2 * n_mix]
    y_ref, idx_ref, gate_ref = refs[2 * n_mix:]
    mix = _dot(o_refs[0][...].astype(BF16), w_refs[0][...])
    for o_ref, w_ref in zip(o_refs[1:], w_refs[1:]):
        mix = mix + _dot(o_ref[...].astype(BF16), w_ref[...])
    y = _layer_norm(DEEPNORM_ALPHA * x_ref[...] + mix, g_ref[...], b_ref[...])
    y_ref[...] = y

    logits = _dot(y.astype(BF16), wr_ref[...]) + br_ref[...]
    lane = lax.broadcasted_iota(I32, logits.shape, 1)
    lane_f = lane.astype(F32)
    idx_out = jnp.zeros(logits.shape, I32)
    val_out = jnp.zeros(logits.shape, F32)
    top = None
    den = jnp.zeros((logits.shape[0], 1), F32)
    for k in range(TOP_K):
        mx = jnp.max(logits, axis=1, keepdims=True)
        ix = jnp.min(jnp.where(logits == mx, lane_f, float(LANES)), axis=1, keepdims=True).astype(I32)
        if k == 0:
            top = mx
        e = jnp.exp(mx - top)
        den = den + e
        idx_out = jnp.where(lane == k, ix, idx_out)
        val_out = jnp.where(lane == k, e, val_out)
        logits = jnp.where(lane == ix, -jnp.inf, logits)
    idx_ref[...] = idx_out
    gate_ref[...] = val_out / den


def _mix_norm_route(x, os_, ws, g, b, wr, br, tm=ROW_TILE):
    n_tok = x.shape[0]
    n = len(os_)
    return pl.pallas_call(
        functools.partial(_mix_norm_kernel, n_mix=n),
        out_shape=[jax.ShapeDtypeStruct((n_tok, D_MODEL), F32),
                   jax.ShapeDtypeStruct((n_tok, LANES), I32),
                   jax.ShapeDtypeStruct((n_tok, LANES), F32)],
        grid=(n_tok // tm,),
        in_specs=[pl.BlockSpec((tm, D_MODEL), lambda i: (i, 0)),
                  pl.BlockSpec((1, D_MODEL), lambda i: (0, 0)),
                  pl.BlockSpec((1, D_MODEL), lambda i: (0, 0)),
                  pl.BlockSpec((D_MODEL, LANES), lambda i: (0, 0)),
                  pl.BlockSpec((1, LANES), lambda i: (0, 0))]
        + [pl.BlockSpec((tm, o.shape[1]), lambda i: (i, 0)) for o in os_]
        + [pl.BlockSpec(w.shape, lambda i: (0, 0)) for w in ws],
        out_specs=[pl.BlockSpec((tm, D_MODEL), lambda i: (i, 0)),
                   pl.BlockSpec((tm, LANES), lambda i: (i, 0)),
                   pl.BlockSpec((tm, LANES), lambda i: (i, 0))],
        compiler_params=_cparams(1),
        name="mix_norm_route",
    )(x, g, b, wr, br, *os_, *ws)


def _moe_kernel(be_ref, nb_ref, x_ref, gate_ref, wgu_ref, bgu_ref, wd_ref, bd_ref, y_ref, wgu_sc, wd_sc):
    i = pl.program_id(0)
    used = i < nb_ref[0]
    new_expert = (i == 0) | (be_ref[i] != be_ref[jnp.maximum(i - 1, 0)])

    @pl.when(used & new_expert)
    def _():
        for r in range(0, D_MODEL, CAST_ROWS):
            wgu_sc[r:r + CAST_ROWS, :] = wgu_ref[0, r:r + CAST_ROWS, :].astype(BF16)
        for r in range(0, D_FF, CAST_ROWS):
            wd_sc[r:r + CAST_ROWS, :] = wd_ref[0, r:r + CAST_ROWS, :].astype(BF16)

    @pl.when(used)
    def _():
        h = _dot(x_ref[...].astype(BF16), wgu_sc[...]) + bgu_ref[0]
        glu = jnp.minimum(h[:, :D_FF], SWIGLU_LIMIT)
        lin = jnp.clip(h[:, D_FF:], -SWIGLU_LIMIT, SWIGLU_LIMIT)
        act = glu * jax.nn.sigmoid(SWIGLU_ALPHA * glu) * (lin + 1.0)
        y = _dot(act.astype(BF16), wd_sc[...]) + bd_ref[0]
        y_ref[...] = y * gate_ref[...]

    @pl.when(i >= nb_ref[0])
    def _():
        y_ref[...] = jnp.zeros(y_ref.shape, y_ref.dtype)


def _moe_blocks(x_sorted, slot_gate, block_expert, n_used, wgu, bgu, wd, bd):
    n_slots = x_sorted.shape[0]
    n_blocks = n_slots // MOE_BLOCK
    grid_spec = pltpu.PrefetchScalarGridSpec(
        num_scalar_prefetch=2,
        grid=(n_blocks,),
        in_specs=[
            pl.BlockSpec((MOE_BLOCK, D_MODEL), lambda i, be, nb: (i, 0)),
            pl.BlockSpec((MOE_BLOCK, 1), lambda i, be, nb: (i, 0)),
            pl.BlockSpec((1, D_MODEL, 2 * D_FF), lambda i, be, nb: (be[i], 0, 0)),
            pl.BlockSpec((1, 1, 2 * D_FF), lambda i, be, nb: (be[i], 0, 0)),
            pl.BlockSpec((1, D_FF, D_MODEL), lambda i, be, nb: (be[i], 0, 0)),
            pl.BlockSpec((1, 1, D_MODEL), lambda i, be, nb: (be[i], 0, 0)),
        ],
        out_specs=pl.BlockSpec((MOE_BLOCK, D_MODEL), lambda i, be, nb: (i, 0)),
        scratch_shapes=[pltpu.VMEM((D_MODEL, 2 * D_FF), BF16), pltpu.VMEM((D_FF, D_MODEL), BF16)],
    )
    return pl.pallas_call(
        _moe_kernel,
        out_shape=jax.ShapeDtypeStruct((n_slots, D_MODEL), F32),
        grid_spec=grid_spec,
        compiler_params=_cparams(1),
        name="moe_blocks",
    )(block_expert, n_used, x_sorted, slot_gate, wgu, bgu, wd, bd)


def _ffn_norm_kernel(x_ref, yg_ref, g_ref, b_ref, o_ref):
    ffn = yg_ref[0] + yg_ref[1] + yg_ref[2] + yg_ref[3]
    o_ref[...] = _layer_norm(DEEPNORM_ALPHA * x_ref[...] + ffn, g_ref[...], b_ref[...])


def _ffn_norm(x, yg, g, b, tm=ROW_TILE):
    n_tok = x.shape[0]
    return pl.pallas_call(
        _ffn_norm_kernel,
        out_shape=jax.ShapeDtypeStruct((n_tok, D_MODEL), F32),
        grid=(n_tok // tm,),
        in_specs=[pl.BlockSpec((tm, D_MODEL), lambda i: (i, 0)),
                  pl.BlockSpec((TOP_K, tm, D_MODEL), lambda i: (0, i, 0)),
                  pl.BlockSpec((1, D_MODEL), lambda i: (0, 0)),
                  pl.BlockSpec((1, D_MODEL), lambda i: (0, 0))],
        out_specs=pl.BlockSpec((tm, D_MODEL), lambda i: (i, 0)),
        compiler_params=_cparams(1),
        name="ffn_norm",
    )(x, yg, g, b)


def _moe_layer(x, top_idx, gate, expert0, wgu, bgu, wd, bd, g, b):
    n_tok = x.shape[0]
    n_assign = n_tok * TOP_K
    e_flat = top_idx.reshape(n_assign)
    onehot = (e_flat[:, None] == jnp.arange(N_EXPERTS, dtype=I32)[None, :]).astype(I32)
    csum = jnp.cumsum(onehot, axis=0)
    counts = csum[-1]
    pos = jnp.take_along_axis(csum, e_flat[:, None], axis=1)[:, 0] - 1
    padded = (counts + MOE_BLOCK - 1) // MOE_BLOCK * MOE_BLOCK
    pad_end = jnp.cumsum(padded)
    pad_start = pad_end - padded
    slot = pad_start[e_flat] + pos
    n_blocks = -(-n_assign // MOE_BLOCK) + N_EXPERTS
    n_slots = n_blocks * MOE_BLOCK
    slot_assign = jnp.full((n_slots,), n_assign, I32).at[slot].set(
        jnp.arange(n_assign, dtype=I32), unique_indices=True, mode='promise_in_bounds')
    slot_tok = jnp.minimum(slot_assign // TOP_K, n_tok)
    slot_gate = jnp.concatenate([gate.reshape(n_assign), jnp.zeros((1,), F32)])[slot_assign]
    n_used = (pad_end[-1] // MOE_BLOCK).astype(I32)
    blk_start = jnp.arange(n_blocks, dtype=I32) * MOE_BLOCK
    block_expert = jnp.minimum(jnp.sum((pad_end[None, :] <= blk_start[:, None]).astype(I32), axis=1), N_EXPERTS - 1)
    last_expert = block_expert[jnp.maximum(n_used - 1, 0)]
    block_expert = jnp.where(jnp.arange(n_blocks) < n_used, block_expert, last_expert)
    x_pad = jnp.concatenate([x, jnp.zeros((1, D_MODEL), F32)], axis=0)
    x_sorted = x_pad[slot_tok]
    y_slots = _moe_blocks(x_sorted, slot_gate[:, None], block_expert + expert0, n_used[None], wgu, bgu, wd, bd)
    yg = y_slots[slot.reshape(n_tok, TOP_K).T]
    return _ffn_norm(x, yg, g, b)


def _alibi_slopes(n_heads):
    return jnp.exp2(-8.0 * jnp.arange(1, n_heads + 1, dtype=F32) / n_heads)


def _block_diag_rows(q, n_heads, width):
    n_seq = q.shape[0] // DEC_SEQ
    q4 = q.reshape(n_seq, DEC_SEQ, n_heads, width)
    eye = jnp.eye(n_heads, dtype=q.dtype)
    out = jnp.einsum('sqhd,hg->shqgd', q4, eye)
    return out.reshape(n_seq, n_heads * DEC_SEQ, n_heads * width)


def kernel(x_prompt, x_sample, cache_kv_even, cache_kv_odd, cache_kidx_odd, page_table,
           w_in_even, w_o_even, w_in_odd, w_o_odd, ln_g, ln_b,
           w_router, b_router, w_gate_up, b_gate_up, w_down, b_down):
    batch, seq, _ = x_prompt.shape
    n_seq, dec_seq, _ = x_sample.shape
    n_p = batch * seq
    n_s = n_seq * dec_seq
    hd = N_HEADS * HEAD_DIM
    x = jnp.concatenate([x_prompt.reshape(n_p, D_MODEL), x_sample.reshape(n_s, D_MODEL)], axis=0)

    slopes_even = _alibi_slopes(H_MOBA)
    slopes_odd = _alibi_slopes(N_HEADS)
    wr = jnp.pad(w_router, ((0, 0), (0, 0), (0, LANES - N_EXPERTS))).astype(BF16)
    br = jnp.pad(b_router, ((0, 0), (0, LANES - N_EXPERTS)), constant_values=-jnp.inf)[:, None, :]
    ln_g4 = ln_g[:, :, None, :]
    ln_b4 = ln_b[:, :, None, :]

    def moe(layer, x_mid, top_idx, gate):
        return _moe_layer(x_mid, top_idx[:, :TOP_K], gate[:, :TOP_K], layer * N_EXPERTS,
                          w_gate_up.reshape(-1, D_MODEL, 2 * D_FF), b_gate_up.reshape(-1, 1, 2 * D_FF),
                          w_down.reshape(-1, D_FF, D_MODEL), b_down.reshape(-1, 1, D_MODEL),
                          ln_g4[layer, 1], ln_b4[layer, 1])

    w_in = w_in_even[0].astype(BF16)
    ws = [w_in[:, :hd], w_in[:, hd:]]
    q_p, kv_p, kv_t = _proj(x, 0, n_p, ws, [w_in[:, hd:].T], seq, tm=PROMPT_PROJ_TILE)
    q_s, kv_s = _proj(x, n_p, n_s, ws)
    o_moba = _moba_prompt(q_p, kv_p, slopes_even, batch, seq)
    o_sb = _sb_prompt(q_p, kv_p, batch, seq)
    qrows = _block_diag_rows(q_s.astype(BF16), N_HEADS, HEAD_DIM)
    o_s = _even_decode(qrows, kv_s, _pool_pages_kv(cache_kv_even), page_table, slopes_even)
    o_all = jnp.concatenate([jnp.concatenate([o_moba, o_sb], axis=1), o_s], axis=0)
    w_o = w_o_even[0].astype(BF16)
    x, top_idx, gate = _mix_norm_route(x, [o_all], [w_o], ln_g4[0, 0], ln_b4[0, 0], wr[0], br[0])
    x = moe(0, x, top_idx, gate)
    new_kv_even_p = _kv_from_feature_major(kv_t, seq)
    new_kv_even_s = kv_s.reshape(1, n_seq, dec_seq, 2, N_HEADS, HEAD_DIM)

    w_in = w_in_odd[0].astype(BF16)
    n_qkv = 3 * hd
    n_qi = IDX_HEADS * IDX_DIM
    w_kidx = w_in[:, n_qkv + n_qi:n_qkv + n_qi + IDX_DIM]
    w_wi = jnp.pad(w_in[:, n_qkv + n_qi + IDX_DIM:], ((0, 0), (0, LANES - IDX_HEADS)))
    ws = [w_in[:, :hd], w_in[:, hd:n_qkv], w_in[:, n_qkv:n_qkv + n_qi],
          jnp.concatenate([w_kidx, w_kidx], axis=1), w_wi]
    q_p, kv_p, qi_p, kidx2_p, wi_p, kv_t = _proj(x, 0, n_p, ws, [w_in[:, hd:n_qkv].T], seq, tm=PROMPT_PROJ_TILE)
    q_s, kv_s, qi_s, kidx2_s, wi_s = _proj(x, n_p, n_s, ws)
    mask = _dsa_select_prompt(qi_p, wi_p, kidx2_p, batch, seq)
    o_p = _dsa_attn_prompt(q_p, kv_p, mask, slopes_odd, batch, seq)
    qrows = _block_diag_rows(q_s.astype(BF16), N_HEADS, HEAD_DIM)
    qirows = qi_s.astype(BF16).reshape(n_seq, DEC_SEQ, IDX_HEADS, IDX_DIM).transpose(0, 2, 1, 3)
    qirows = qirows.reshape(n_seq, IDX_HEADS * DEC_SEQ, IDX_DIM)
    kidx_s = kidx2_s[:, :IDX_DIM]
    pool_idx = jnp.transpose(cache_kidx_odd[0], (0, 2, 1))
    o_s = _odd_decode(qrows, qirows, wi_s, kv_s, kidx_s, _pool_pages_kv(cache_kv_odd), pool_idx,
                      page_table, slopes_odd)
    o_all = jnp.concatenate([o_p, o_s], axis=0)
    w_o = w_o_odd[0].astype(BF16)
    x, top_idx, gate = _mix_norm_route(x, [o_all], [w_o], ln_g4[1, 0], ln_b4[1, 0], wr[1], br[1])
    x = moe(1, x, top_idx, gate)
    new_kv_odd_p = _kv_from_feature_major(kv_t, seq)
    new_kv_odd_s = kv_s.reshape(1, n_seq, dec_seq, 2, N_HEADS, HEAD_DIM)
    new_kidx_p = kidx2_p[:, :IDX_DIM].reshape(1, batch, seq, IDX_DIM)
    new_kidx_s = kidx_s.reshape(1, n_seq, dec_seq, IDX_DIM)

    y_prompt = x[:n_p].reshape(batch, seq, D_MODEL)
    y_sample = x[n_p:].reshape(n_seq, dec_seq, D_MODEL)
    return (y_prompt, y_sample, new_kv_even_p, new_kv_even_s, new_kv_odd_p, new_kv_odd_s,
            new_kidx_p, new_kidx_s)
```

```python
import functools

import jax
import jax.numpy as jnp
from jax import lax
from jax.experimental import pallas as pl
from jax.experimental.pallas import tpu as pltpu

F32 = jnp.float32
BF16 = jnp.bfloat16
I32 = jnp.int32

D_MODEL = 1024
HEAD_DIM = 64
N_HEADS = 16
H_MOBA = 8
MOBA_BLOCK = 256
MOBA_TOPK = 3
DSA_TOPK = 256
IDX_HEADS = 8
IDX_DIM = 64
N_EXPERTS = 32
TOP_K = 4
D_FF = 1024
SWIGLU_LIMIT = 7.0
SWIGLU_ALPHA = 1.702
LN_EPS = 1e-5
DEPTH = 2
DEEPNORM_ALPHA = (2 * DEPTH) ** 0.25
PAGE_SIZE = 128
PAST_LEN = 2048
N_PAGES = PAST_LEN // PAGE_SIZE
DEC_SEQ = 8

LANES = 128
TQ = 256
PAIRS = 2
DSA_PAIRS = 4
MOE_BLOCK = 256
CAST_ROWS = 128
ROW_TILE = 512
PROMPT_PROJ_TILE = 256
NEG_MASKED = -2e30
NEG_INIT = -1e30
INT_MIN = -2 ** 31
SB_DEAD_TAIL = -104.0
VMEM_LIMIT = 56 * 1024 * 1024
SCALE = HEAD_DIM ** -0.5


def _cparams(n_grid):
    return pltpu.CompilerParams(dimension_semantics=("arbitrary",) * n_grid,
                                vmem_limit_bytes=VMEM_LIMIT)


def _dot(a, b):
    return jnp.dot(a, b, preferred_element_type=F32)


def _dot_nt(a, b):
    return lax.dot_general(a, b, (((1,), (1,)), ((), ())), preferred_element_type=F32)


def _split_dot(x, tri):
    hi = x.astype(BF16)
    lo = (x - hi.astype(F32)).astype(BF16)
    return _dot(hi, tri) + _dot(lo, tri)


def _log_sigmoid_neg(z):
    return -(jnp.maximum(z, 0.0) + jnp.log(1.0 + jnp.exp(-jnp.abs(z))))


def _proj_kernel(x_ref, *refs, n_out, n_out_t):
    n_w = n_out + n_out_t
    xb = x_ref[...].astype(BF16)
    for w_ref, o_ref in zip(refs[:n_out], refs[n_w:n_w + n_out]):
        o_ref[...] = _dot(xb, w_ref[...])
    for wt_ref, ot_ref in zip(refs[n_out:n_w], refs[n_w + n_out:]):
        ot_ref[0] = _dot_nt(wt_ref[...], xb)


def _proj(x, row0, n_rows, ws, wts=(), seq=None, tm=ROW_TILE):
    assert row0 % tm == 0 and n_rows % tm == 0
    blk0 = row0 // tm
    out_shape = [jax.ShapeDtypeStruct((n_rows, w.shape[1]), F32) for w in ws]
    out_specs = [pl.BlockSpec((tm, w.shape[1]), lambda i: (i, 0)) for w in ws]
    if wts:
        per_seq = seq // tm
        out_shape += [jax.ShapeDtypeStruct((n_rows // seq, wt.shape[0], seq), F32) for wt in wts]
        out_specs += [pl.BlockSpec((1, wt.shape[0], tm), lambda i: (i // per_seq, 0, i % per_seq)) for wt in wts]
    return pl.pallas_call(
        functools.partial(_proj_kernel, n_out=len(ws), n_out_t=len(wts)),
        out_shape=out_shape,
        grid=(n_rows // tm,),
        in_specs=[pl.BlockSpec((tm, D_MODEL), lambda i: (i + blk0, 0))]
        + [pl.BlockSpec(w.shape, lambda i: (0, 0)) for w in list(ws) + list(wts)],
        out_specs=out_specs,
        compiler_params=_cparams(1),
        name="proj",
    )(x, *ws, *wts)


def _kv_from_feature_major(kv_t, seq):
    batch = kv_t.shape[0]
    return jnp.transpose(kv_t.reshape(batch, 2, N_HEADS, HEAD_DIM, seq), (0, 4, 1, 2, 3))[None]


def _head_queries(q, scale):
    lane = lax.broadcasted_iota(I32, q.shape, 1)
    qs = q * scale
    return [jnp.where((lane >= hh * HEAD_DIM) & (lane < (hh + 1) * HEAD_DIM), qs, 0.0).astype(BF16)
            for hh in range(2)]


def _softmax_tile(s, kv_b, state):
    m, l, acc = state
    m_new = jnp.maximum(m, jnp.max(s, axis=1, keepdims=True))
    p = jnp.exp(s - m_new)
    alpha = jnp.exp(m - m_new)
    l = alpha * l + jnp.sum(p, axis=1, keepdims=True)
    acc = alpha * acc + _dot(p.astype(BF16), kv_b)
    return m_new, l, acc


def _init_state():
    return (jnp.full((TQ, 1), NEG_INIT, F32), jnp.zeros((TQ, 1), F32), jnp.zeros((TQ, LANES), F32))


def _finish_pair(states):
    lane = lax.broadcasted_iota(I32, (TQ, LANES), 1)
    o0 = states[0][2] / states[0][1]
    o1 = states[1][2] / states[1][1]
    return jnp.where(lane < HEAD_DIM, o0, o1)


def _moba_kernel(slopes_ref, q_ref, k_ref, v_ref, o_ref, kmean_sc, *, nt):
    g = pl.program_id(1)
    i = pl.program_id(2)

    @pl.when(i == 0)
    def _():
        for n in range(nt):
            kmean_sc[n:n + 1, :] = jnp.sum(k_ref[n * TQ:(n + 1) * TQ, :], axis=0, keepdims=True) * (1.0 / MOBA_BLOCK)

    blk = lax.broadcasted_iota(I32, (TQ, nt), 1)
    past = blk < i
    row = lax.broadcasted_iota(I32, (TQ, TQ), 0)
    col = lax.broadcasted_iota(I32, (TQ, TQ), 1)
    rel = (row - col).astype(F32)

    q_att, sels, slopes = [], [], []
    for pr in range(PAIRS):
        q = q_ref[:, pr * LANES:(pr + 1) * LANES]
        km = kmean_sc[:, pr * LANES:(pr + 1) * LANES].astype(BF16)
        q_att.extend(_head_queries(q, SCALE))
        for hh, q_gate in enumerate(_head_queries(q, 1.0)):
            gate = jnp.where(past, _dot_nt(q_gate, km), -jnp.inf)
            rank = jnp.zeros((TQ, nt), I32)
            for n2 in range(nt):
                gn = gate[:, n2:n2 + 1]
                rank = rank + ((gn > gate) | ((gn == gate) & (n2 < blk))).astype(I32)
            sels.append((past & (rank < MOBA_TOPK)).astype(F32))
            slopes.append(slopes_ref[(g * PAIRS + pr) * 2 + hh])

    def kv_tile(off, pr):
        return (k_ref[pl.ds(off, TQ), pr * LANES:(pr + 1) * LANES].astype(BF16),
                v_ref[pl.ds(off, TQ), pr * LANES:(pr + 1) * LANES].astype(BF16))

    start = pl.multiple_of(i * TQ, TQ)
    init = []
    for pr in range(PAIRS):
        kb, vb = kv_tile(start, pr)
        for hh in range(2):
            s = _dot_nt(q_att[2 * pr + hh], kb) - slopes[2 * pr + hh] * rel
            s = jnp.where(col <= row, s, NEG_MASKED)
            init.extend(_softmax_tile(s, vb, _init_state()))

    def body(j, carry):
        off = pl.multiple_of(j * TQ, TQ)
        dist = rel + ((i - j) * TQ).astype(F32)
        out = []
        for pr in range(PAIRS):
            kb, vb = kv_tile(off, pr)
            for hh in range(2):
                h = 2 * pr + hh
                sel_j = jnp.sum(jnp.where(blk == j, sels[h], 0.0), axis=1, keepdims=True) > 0.0
                s = _dot_nt(q_att[h], kb) - slopes[h] * dist
                s = jnp.where(sel_j, s, NEG_MASKED)
                out.extend(_softmax_tile(s, vb, carry[3 * h:3 * h + 3]))
        return tuple(out)

    carry = lax.fori_loop(0, i, body, tuple(init))
    for pr in range(PAIRS):
        o_ref[:, pr * LANES:(pr + 1) * LANES] = _finish_pair([carry[6 * pr:6 * pr + 3], carry[6 * pr + 3:6 * pr + 6]])


def _moba_prompt(q, kv, slopes, batch, seq):
    nt = seq // TQ
    width = PAIRS * LANES
    v_blk0 = N_HEADS * HEAD_DIM // width
    return pl.pallas_call(
        functools.partial(_moba_kernel, nt=nt),
        out_shape=jax.ShapeDtypeStruct((batch * seq, H_MOBA * HEAD_DIM), F32),
        grid=(batch, H_MOBA // (2 * PAIRS), nt),
        in_specs=[
            pl.BlockSpec(memory_space=pltpu.SMEM),
            pl.BlockSpec((TQ, width), lambda b, g, i: (b * nt + i, g)),
            pl.BlockSpec((seq, width), lambda b, g, i: (b, g)),
            pl.BlockSpec((seq, width), lambda b, g, i: (b, v_blk0 + g)),
        ],
        out_specs=pl.BlockSpec((TQ, width), lambda b, g, i: (b * nt + i, g)),
        scratch_shapes=[pltpu.VMEM((nt, width), F32)],
        compiler_params=_cparams(3),
        name="moba_prompt",
    )(slopes, q, kv, kv)


def _sb_tile(qh, kb, vb, tri, carry, acc, strict_mask):
    z = _dot_nt(qh, kb)
    lk = _log_sigmoid_neg(z)
    if strict_mask is not None:
        lk = jnp.where(strict_mask, lk, 0.0)
    incl = _split_dot(lk, tri)
    w = jnp.exp(z + incl + carry)
    if strict_mask is not None:
        w = jnp.where(strict_mask, w, 0.0)
    return carry + incl[:, 0:1], acc + _dot(w.astype(BF16), vb)


def _sb_kernel(q_ref, k_ref, v_ref, o_ref):
    i = pl.program_id(2)
    q_att = _head_queries(q_ref[...], SCALE)
    row = lax.broadcasted_iota(I32, (TQ, TQ), 0)
    col = lax.broadcasted_iota(I32, (TQ, TQ), 1)
    tri = (row >= col).astype(BF16)
    strict = col < row

    start = pl.multiple_of(i * TQ, TQ)
    kb = k_ref[pl.ds(start, TQ), :].astype(BF16)
    vb = v_ref[pl.ds(start, TQ), :].astype(BF16)
    init = []
    for hh in range(2):
        init.extend(_sb_tile(q_att[hh], kb, vb, tri, jnp.zeros((TQ, 1), F32),
                             jnp.zeros((TQ, LANES), F32), strict))

    def live(tail0, tail1):
        return jnp.max(jnp.maximum(tail0, tail1)) > SB_DEAD_TAIL

    def cond(state):
        return (state[0] < i) & state[1]

    def body(state):
        t, carry = state[0], state[2:]
        j = i - 1 - t
        off = pl.multiple_of(j * TQ, TQ)
        kb = k_ref[pl.ds(off, TQ), :].astype(BF16)
        vb = v_ref[pl.ds(off, TQ), :].astype(BF16)
        out = []
        for hh in range(2):
            out.extend(_sb_tile(q_att[hh], kb, vb, tri, carry[2 * hh], carry[2 * hh + 1], None))
        return (t + 1, live(out[0], out[2])) + tuple(out)

    state = lax.while_loop(cond, body, (jnp.int32(0), live(init[0], init[2])) + tuple(init))
    lane = lax.broadcasted_iota(I32, (TQ, LANES), 1)
    o_ref[...] = jnp.where(lane < HEAD_DIM, state[3], state[5])


def _sb_prompt(q, kv, batch, seq):
    nt = seq // TQ
    n_sb = N_HEADS - H_MOBA
    hp0 = H_MOBA // 2
    return pl.pallas_call(
        _sb_kernel,
        out_shape=jax.ShapeDtypeStruct((batch * seq, n_sb * HEAD_DIM), F32),
        grid=(batch, n_sb // 2, nt),
        in_specs=[
            pl.BlockSpec((TQ, LANES), lambda b, hp, i: (b * nt + i, hp0 + hp)),
            pl.BlockSpec((seq, LANES), lambda b, hp, i: (b, hp0 + hp)),
            pl.BlockSpec((seq, LANES), lambda b, hp, i: (b, N_HEADS // 2 + hp0 + hp)),
        ],
        out_specs=pl.BlockSpec((TQ, LANES), lambda b, hp, i: (b * nt + i, hp)),
        compiler_params=_cparams(3),
        name="sb_prompt",
    )(q, kv, kv)


def _ordered_key(score):
    bits = lax.bitcast_convert_type(score + 0.0, I32)
    return jnp.where(bits < 0, bits ^ 0x7FFFFFFF, bits)


def _kth_largest_key(count_ge, rows, k):
    t = jnp.where(count_ge(jnp.zeros((rows, 1), I32)) >= k, 0, INT_MIN).astype(I32)

    def body(b, t):
        cand = t | (1 << (30 - b))
        return jnp.where(count_ge(cand) >= k, cand, t)

    return lax.fori_loop(0, 31, body, t)


def _dsa_select_kernel(qi_ref, wi_ref, kidx_ref, mask_ref, key_sc, *, nt):
    i = pl.program_id(1)
    qi = qi_ref[...]
    wi = wi_ref[...]
    row = lax.broadcasted_iota(I32, (TQ, TQ), 0)
    col = lax.broadcasted_iota(I32, (TQ, TQ), 1)
    qh = []
    for p in range(IDX_HEADS // 2):
        qh.extend(_head_queries(qi[:, p * LANES:(p + 1) * LANES], 1.0))
    wcol = [wi[:, h:h + 1] for h in range(IDX_HEADS)]

    def score_tile(j):
        off = pl.multiple_of(j * TQ, TQ)
        kb = kidx_ref[pl.ds(off, TQ), :].astype(BF16)
        score = jnp.zeros((TQ, TQ), F32)
        for h in range(IDX_HEADS):
            score = score + wcol[h] * jnp.maximum(_dot_nt(qh[h], kb), 0.0)
        return _ordered_key(score)

    def fill(j, c):
        off = pl.multiple_of(j * TQ, TQ)
        key_sc[:, pl.ds(off, TQ)] = score_tile(j)
        return c

    lax.fori_loop(0, i, fill, 0)
    start = pl.multiple_of(i * TQ, TQ)
    key_sc[:, pl.ds(start, TQ)] = jnp.where(col <= row, score_tile(i), INT_MIN)

    @pl.when(i % 2 == 0)
    def _():
        key_sc[:, pl.ds(pl.multiple_of((i + 1) * TQ, TQ), TQ)] = jnp.full((TQ, TQ), INT_MIN, I32)

    def count(pred):
        def body(j, acc):
            off = pl.multiple_of(j * 2 * TQ, 2 * TQ)
            hit = jnp.where(pred(key_sc[:, pl.ds(off, 2 * TQ)]), 1.0, 0.0)
            for t in range(2 * TQ // LANES):
                acc = acc + hit[:, t * LANES:(t + 1) * LANES]
            return acc
        acc = lax.fori_loop(0, (i + 2) // 2, body, jnp.zeros((TQ, LANES), F32))
        return jnp.sum(acc, axis=1, keepdims=True)

    thr = _kth_largest_key(lambda cand: count(lambda k: k >= cand), TQ, DSA_TOPK)
    need = DSA_TOPK - count(lambda k: k > thr)
    tri = (row <= col).astype(BF16)

    mask_ref[...] = jnp.zeros(mask_ref.shape, mask_ref.dtype)

    def emit(j, run):
        off = pl.multiple_of(j * TQ, TQ)
        key = key_sc[:, pl.ds(off, TQ)]
        eq = key == thr
        eqf = jnp.where(eq, 1.0, 0.0)
        prefix = _dot(eqf.astype(BF16), tri) + run
        keep = ((key > thr) | (eq & (prefix <= need))) & (key != INT_MIN)
        mask_ref[:, pl.ds(off, TQ)] = jnp.where(keep, 1.0, 0.0).astype(mask_ref.dtype)
        return run + jnp.sum(eqf, axis=1, keepdims=True)

    lax.fori_loop(0, i + 1, emit, jnp.zeros((TQ, 1), F32))


def _dsa_select_prompt(qi, wi, kidx2, batch, seq):
    nt = seq // TQ
    assert nt % 2 == 0, "the counting loop pads to pairs of key tiles"
    return pl.pallas_call(
        functools.partial(_dsa_select_kernel, nt=nt),
        out_shape=jax.ShapeDtypeStruct((batch * seq, seq), BF16),
        grid=(batch, nt),
        in_specs=[
            pl.BlockSpec((TQ, IDX_HEADS * IDX_DIM), lambda b, i: (b * nt + i, 0)),
            pl.BlockSpec((TQ, LANES), lambda b, i: (b * nt + i, 0)),
            pl.BlockSpec((seq, LANES), lambda b, i: (b, 0)),
        ],
        out_specs=pl.BlockSpec((TQ, seq), lambda b, i: (b * nt + i, 0)),
        scratch_shapes=[pltpu.VMEM((TQ, seq), I32)],
        compiler_params=_cparams(2),
        name="dsa_select_prompt",
    )(qi, wi, kidx2)


def _dsa_attn_kernel(slopes_ref, q_ref, k_ref, v_ref, mask_ref, o_ref):
    g = pl.program_id(1)
    i = pl.program_id(2)
    row = lax.broadcasted_iota(I32, (TQ, TQ), 0)
    col = lax.broadcasted_iota(I32, (TQ, TQ), 1)
    rel = (row - col).astype(F32)
    q_att, slopes = [], []
    for pr in range(DSA_PAIRS):
        q_att.extend(_head_queries(q_ref[:, pr * LANES:(pr + 1) * LANES], SCALE))
        slopes.extend(slopes_ref[(g * DSA_PAIRS + pr) * 2 + hh] for hh in range(2))

    def body(j, carry):
        off = pl.multiple_of(j * TQ, TQ)
        keep = mask_ref[:, pl.ds(off, TQ)] > 0
        dist = rel + ((i - j) * TQ).astype(F32)
        out = []
        for pr in range(DSA_PAIRS):
            kb = k_ref[pl.ds(off, TQ), pr * LANES:(pr + 1) * LANES].astype(BF16)
            vb = v_ref[pl.ds(off, TQ), pr * LANES:(pr + 1) * LANES].astype(BF16)
            for hh in range(2):
                h = 2 * pr + hh
                s = _dot_nt(q_att[h], kb) - slopes[h] * dist
                s = jnp.where(keep, s, NEG_MASKED)
                out.extend(_softmax_tile(s, vb, carry[3 * h:3 * h + 3]))
        return tuple(out)

    carry = lax.fori_loop(0, i + 1, body, _init_state() * (2 * DSA_PAIRS))
    for pr in range(DSA_PAIRS):
        o_ref[:, pr * LANES:(pr + 1) * LANES] = _finish_pair([carry[6 * pr:6 * pr + 3], carry[6 * pr + 3:6 * pr + 6]])


def _dsa_attn_prompt(q, kv, mask, slopes, batch, seq):
    nt = seq // TQ
    width = DSA_PAIRS * LANES
    v_blk0 = N_HEADS * HEAD_DIM // width
    return pl.pallas_call(
        _dsa_attn_kernel,
        out_shape=jax.ShapeDtypeStruct((batch * seq, N_HEADS * HEAD_DIM), F32),
        grid=(batch, N_HEADS // (2 * DSA_PAIRS), nt),
        in_specs=[
            pl.BlockSpec(memory_space=pltpu.SMEM),
            pl.BlockSpec((TQ, width), lambda b, g, i: (b * nt + i, g)),
            pl.BlockSpec((seq, width), lambda b, g, i: (b, g)),
            pl.BlockSpec((seq, width), lambda b, g, i: (b, v_blk0 + g)),
            pl.BlockSpec((TQ, seq), lambda b, g, i: (b * nt + i, 0)),
        ],
        out_specs=pl.BlockSpec((TQ, width), lambda b, g, i: (b * nt + i, g)),
        compiler_params=_cparams(3),
        name="dsa_attn_prompt",
    )(slopes, q, kv, kv, mask)


N_CHUNK = N_PAGES + 1
ROWS = N_HEADS * DEC_SEQ


def _row_slopes(slopes_ref, n_heads, rows):
    r = lax.broadcasted_iota(I32, (rows, 1), 0) // DEC_SEQ
    out = jnp.zeros((rows, 1), F32)
    for h in range(n_heads):
        out = jnp.where(r == h, slopes_ref[h], out)
    return out


def _pad_new(x8):
    return jnp.concatenate([x8, jnp.zeros((PAGE_SIZE - DEC_SEQ, x8.shape[1]), x8.dtype)], axis=0)


def _diag_heads(acc, n_heads):
    lane = lax.broadcasted_iota(I32, (DEC_SEQ, acc.shape[1]), 1) // HEAD_DIM
    out = jnp.zeros((DEC_SEQ, acc.shape[1]), F32)
    for h in range(n_heads):
        out = jnp.where(lane == h, acc[h * DEC_SEQ:(h + 1) * DEC_SEQ, :], out)
    return out


def _scores(qrows, page_ref, new_rows):
    if page_ref is not None:
        return _dot(qrows, page_ref[0, 0].astype(BF16))
    return _dot_nt(qrows, _pad_new(new_rows).astype(BF16))


def _weighted_values(p, page_ref, new_rows, lo, n):
    if page_ref is not None:
        return _dot_nt(p, page_ref[0, 1, lo:lo + n, :].astype(BF16))
    return _dot(p, _pad_new(new_rows[:, lo:lo + n]).astype(BF16))


def _even_decode_kernel(pt_ref, slopes_ref, qrows_ref, new_ref, *refs):
    page_refs = list(refs[:N_PAGES]) + [None]
    o_ref = refs[N_PAGES]
    s_sc, p_sc, km_sc = refs[N_PAGES + 1:]
    half = H_MOBA * HEAD_DIM
    mrows = H_MOBA * DEC_SEQ
    qrows = qrows_ref[0]
    newkv = new_ref[...]
    new_k, new_v = newkv[:, 0:D_MODEL], newkv[:, D_MODEL:2 * D_MODEL]

    km_sc[...] = jnp.zeros(km_sc.shape, F32)
    for c in range(N_CHUNK):
        s_sc[:, c * LANES:(c + 1) * LANES] = _scores(qrows, page_refs[c], new_k) * SCALE
        if c < N_PAGES:
            part = jnp.sum(page_refs[c][0, 0, 0:half, :], axis=1, keepdims=True)
            n = c // 2
            if c % 2 == 0:
                km_sc[:, n:n + 1] = part
            else:
                km_sc[:, n:n + 1] = (km_sc[:, n:n + 1] + part) * (1.0 / MOBA_BLOCK)

    nb = N_PAGES // 2
    nb_pad = km_sc.shape[1]
    blk = lax.broadcasted_iota(I32, (mrows, nb_pad), 1)
    gate = jnp.where(blk < nb, _dot(qrows[0:mrows, 0:half], km_sc[...].astype(BF16)), -jnp.inf)
    rank = jnp.zeros((mrows, nb_pad), I32)
    for n2 in range(nb):
        gn = gate[:, n2:n2 + 1]
        rank = rank + ((gn > gate) | ((gn == gate) & (n2 < blk))).astype(I32)
    sel = jnp.where((blk < nb) & (rank < MOBA_TOPK), 1.0, 0.0)
    slope = _row_slopes(slopes_ref, H_MOBA, mrows)
    qi = lax.broadcasted_iota(I32, (mrows, LANES), 0) % DEC_SEQ
    cl = lax.broadcasted_iota(I32, (mrows, LANES), 1)

    def moba_scores(c):
        s = s_sc[0:mrows, c * LANES:(c + 1) * LANES]
        if c < N_PAGES:
            dist = (PAST_LEN + qi - (c * PAGE_SIZE + cl)).astype(F32)
            keep = jnp.broadcast_to(sel[:, c // 2:c // 2 + 1], (mrows, LANES)) > 0.5
        else:
            dist = (qi - cl).astype(F32)
            keep = cl <= qi
        return jnp.where(keep, s - slope * dist, NEG_MASKED)

    m = jnp.full((mrows, 1), NEG_INIT, F32)
    for c in range(N_CHUNK):
        sc = moba_scores(c)
        p_sc[:, c * LANES:(c + 1) * LANES] = sc
        m = jnp.maximum(m, jnp.max(sc, axis=1, keepdims=True))
    l = jnp.zeros((mrows, 1), F32)
    for c in range(N_CHUNK):
        p = jnp.exp(p_sc[:, c * LANES:(c + 1) * LANES] - m)
        p_sc[:, c * LANES:(c + 1) * LANES] = p
        l = l + jnp.sum(p, axis=1, keepdims=True)
    inv = 1.0 / l
    acc = jnp.zeros((mrows, half), F32)
    for c in range(N_CHUNK):
        p = (p_sc[:, c * LANES:(c + 1) * LANES] * inv).astype(BF16)
        acc = acc + _weighted_values(p, page_refs[c], new_v, 0, half)
    o_ref[:, 0:half] = _diag_heads(acc, H_MOBA)

    r2 = lax.broadcasted_iota(I32, (LANES, LANES), 0)
    c2 = lax.broadcasted_iota(I32, (LANES, LANES), 1)
    tri = (r2 >= c2).astype(BF16)
    carry = jnp.zeros((mrows, 1), F32)
    acc = jnp.zeros((mrows, half), F32)
    for c in range(N_CHUNK - 1, -1, -1):
        z = s_sc[mrows:2 * mrows, c * LANES:(c + 1) * LANES]
        lk = _log_sigmoid_neg(z)
        if c == N_PAGES:
            strict = cl < qi
            lk = jnp.where(strict, lk, 0.0)
        incl = _split_dot(lk, tri)
        w = jnp.exp(z + incl + carry)
        if c == N_PAGES:
            w = jnp.where(strict, w, 0.0)
        carry = carry + incl[:, 0:1]
        acc = acc + _weighted_values(w.astype(BF16), page_refs[c], new_v, half, half)
    o_ref[:, half:2 * half] = _diag_heads(acc, N_HEADS - H_MOBA)


def _page_specs(*tail):
    def spec(p):
        return pl.BlockSpec((1,) + tail + (PAGE_SIZE,), lambda s, pt: (pt[s, p],) + (0,) * (len(tail) + 1))
    return [spec(p) for p in range(N_PAGES)]


def _pool_pages_kv(cache):
    pool = jnp.transpose(cache[0], (0, 2, 3, 4, 1))
    return pool.reshape(pool.shape[0], 2, D_MODEL, PAGE_SIZE)


def _even_decode(qrows, newkv, pool, page_table, slopes):
    n_seq = qrows.shape[0]
    width = N_CHUNK * LANES
    grid_spec = pltpu.PrefetchScalarGridSpec(
        num_scalar_prefetch=1,
        grid=(n_seq,),
        in_specs=[
            pl.BlockSpec(memory_space=pltpu.SMEM),
            pl.BlockSpec((1, ROWS, D_MODEL), lambda s, pt: (s, 0, 0)),
            pl.BlockSpec((DEC_SEQ, 2 * D_MODEL), lambda s, pt: (s, 0)),
        ] + _page_specs(2, D_MODEL),
        out_specs=pl.BlockSpec((DEC_SEQ, D_MODEL), lambda s, pt: (s, 0)),
        scratch_shapes=[pltpu.VMEM((ROWS, width), F32),
                        pltpu.VMEM((H_MOBA * DEC_SEQ, width), F32),
                        pltpu.VMEM((H_MOBA * HEAD_DIM, LANES), F32)],
    )
    return pl.pallas_call(
        _even_decode_kernel,
        out_shape=jax.ShapeDtypeStruct((n_seq * DEC_SEQ, D_MODEL), F32),
        grid_spec=grid_spec,
        compiler_params=_cparams(1),
        name="even_decode",
    )(page_table, slopes, qrows, newkv, *([pool] * N_PAGES))


def _odd_decode_kernel(pt_ref, slopes_ref, qrows_ref, qirows_ref, wi_ref, new_ref, newidx_ref, *refs):
    page_refs = list(refs[:N_PAGES]) + [None]
    idx_refs = refs[N_PAGES:2 * N_PAGES]
    o_ref = refs[2 * N_PAGES]
    key_sc, s_sc = refs[2 * N_PAGES + 1:]
    qrows = qrows_ref[0]
    qirows = qirows_ref[0]
    wi = wi_ref[...]
    newkv = new_ref[...]
    new_k, new_v = newkv[:, 0:D_MODEL], newkv[:, D_MODEL:2 * D_MODEL]
    qi8 = lax.broadcasted_iota(I32, (DEC_SEQ, LANES), 0)
    cl8 = lax.broadcasted_iota(I32, (DEC_SEQ, LANES), 1)

    for c in range(N_CHUNK):
        if c < N_PAGES:
            dots = _dot(qirows, idx_refs[c][0].astype(BF16))
        else:
            dots = _dot_nt(qirows, _pad_new(newidx_ref[...]).astype(BF16))
        dots = jnp.maximum(dots, 0.0)
        score = jnp.zeros((DEC_SEQ, LANES), F32)
        for h in range(IDX_HEADS):
            score = score + wi[:, h:h + 1] * dots[h * DEC_SEQ:(h + 1) * DEC_SEQ, :]
        key = _ordered_key(score)
        if c == N_PAGES:
            key = jnp.where(cl8 <= qi8, key, INT_MIN)
        key_sc[:, c * LANES:(c + 1) * LANES] = key

    def count(pred):
        return jnp.sum(jnp.where(pred(key_sc[...]), 1.0, 0.0), axis=1, keepdims=True)

    thr = _kth_largest_key(lambda cand: count(lambda k: k >= cand), DEC_SEQ, DSA_TOPK)
    need = DSA_TOPK - count(lambda k: k > thr)
    r2 = lax.broadcasted_iota(I32, (LANES, LANES), 0)
    c2 = lax.broadcasted_iota(I32, (LANES, LANES), 1)
    tri = (r2 <= c2).astype(BF16)
    slope = _row_slopes(slopes_ref, N_HEADS, ROWS)
    qi = lax.broadcasted_iota(I32, (ROWS, LANES), 0) % DEC_SEQ
    cl = lax.broadcasted_iota(I32, (ROWS, LANES), 1)

    run = jnp.zeros((DEC_SEQ, 1), F32)
    m = jnp.full((ROWS, 1), NEG_INIT, F32)
    for c in range(N_CHUNK):
        key = key_sc[:, c * LANES:(c + 1) * LANES]
        eq = key == thr
        eqf = jnp.where(eq, 1.0, 0.0)
        prefix = _dot(eqf.astype(BF16), tri) + run
        run = run + jnp.sum(eqf, axis=1, keepdims=True)
        keep8 = ((key > thr) | (eq & (prefix <= need))) & (key != INT_MIN)
        keep = jnp.concatenate([jnp.where(keep8, 1.0, 0.0)] * N_HEADS, axis=0) > 0.0
        if c < N_PAGES:
            dist = (PAST_LEN + qi - (c * PAGE_SIZE + cl)).astype(F32)
        else:
            dist = (qi - cl).astype(F32)
        s = jnp.where(keep, _scores(qrows, page_refs[c], new_k) * SCALE - slope * dist, NEG_MASKED)
        s_sc[:, c * LANES:(c + 1) * LANES] = s
        m = jnp.maximum(m, jnp.max(s, axis=1, keepdims=True))

    l = jnp.zeros((ROWS, 1), F32)
    for c in range(N_CHUNK):
        p = jnp.exp(s_sc[:, c * LANES:(c + 1) * LANES] - m)
        s_sc[:, c * LANES:(c + 1) * LANES] = p
        l = l + jnp.sum(p, axis=1, keepdims=True)
    inv = 1.0 / l
    acc = jnp.zeros((ROWS, D_MODEL), F32)
    for c in range(N_CHUNK):
        p = (s_sc[:, c * LANES:(c + 1) * LANES] * inv).astype(BF16)
        acc = acc + _weighted_values(p, page_refs[c], new_v, 0, D_MODEL)
    o_ref[...] = _diag_heads(acc, N_HEADS)


def _odd_decode(qrows, qirows, wi, newkv, newidx, pool, pool_idx, page_table, slopes):
    n_seq = qrows.shape[0]
    width = N_CHUNK * LANES
    grid_spec = pltpu.PrefetchScalarGridSpec(
        num_scalar_prefetch=1,
        grid=(n_seq,),
        in_specs=[
            pl.BlockSpec(memory_space=pltpu.SMEM),
            pl.BlockSpec((1, ROWS, D_MODEL), lambda s, pt: (s, 0, 0)),
            pl.BlockSpec((1, IDX_HEADS * DEC_SEQ, IDX_DIM), lambda s, pt: (s, 0, 0)),
            pl.BlockSpec((DEC_SEQ, LANES), lambda s, pt: (s, 0)),
            pl.BlockSpec((DEC_SEQ, 2 * D_MODEL), lambda s, pt: (s, 0)),
            pl.BlockSpec((DEC_SEQ, IDX_DIM), lambda s, pt: (s, 0)),
        ] + _page_specs(2, D_MODEL) + _page_specs(IDX_DIM),
        out_specs=pl.BlockSpec((DEC_SEQ, D_MODEL), lambda s, pt: (s, 0)),
        scratch_shapes=[pltpu.VMEM((DEC_SEQ, width), I32),
                        pltpu.VMEM((ROWS, width), F32)],
    )
    return pl.pallas_call(
        _odd_decode_kernel,
        out_shape=jax.ShapeDtypeStruct((n_seq * DEC_SEQ, D_MODEL), F32),
        grid_spec=grid_spec,
        compiler_params=_cparams(1),
        name="odd_decode",
    )(page_table, slopes, qrows, qirows, wi, newkv, newidx, *([pool] * N_PAGES), *([pool_idx] * N_PAGES))


def _layer_norm(x, g, b):
    mu = jnp.mean(x, axis=-1, keepdims=True)
    xc = x - mu
    var = jnp.mean(xc * xc, axis=-1, keepdims=True)
    return xc * lax.rsqrt(var + LN_EPS) * g + b


def _mix_norm_kernel(x_ref, g_ref, b_ref, wr_ref, br_ref, *refs, n_mix):
    o_refs = refs[:n_mix]
    w_refs = refs[n_mix:2 * n_mix]
    y_ref, idx_ref, gate_ref = refs[2 * n_mix:]
    mix = _dot(o_refs[0][...].astype(BF16), w_refs[0][...])
    for o_ref, w_ref in zip(o_refs[1:], w_refs[1:]):
        mix = mix + _dot(o_ref[...].astype(BF16), w_ref[...])
    y = _layer_norm(DEEPNORM_ALPHA * x_ref[...] + mix, g_ref[...], b_ref[...])
    y_ref[...] = y

    logits = _dot(y.astype(BF16), wr_ref[...]) + br_ref[...]
    lane = lax.broadcasted_iota(I32, logits.shape, 1)
    lane_f = lane.astype(F32)
    idx_out = jnp.zeros(logits.shape, I32)
    val_out = jnp.zeros(logits.shape, F32)
    top = None
    den = jnp.zeros((logits.shape[0], 1), F32)
    for k in range(TOP_K):
        mx = jnp.max(logits, axis=1, keepdims=True)
        ix = jnp.min(jnp.where(logits == mx, lane_f, float(LANES)), axis=1, keepdims=True).astype(I32)
        if k == 0:
            top = mx
        e = jnp.exp(mx - top)
        den = den + e
        idx_out = jnp.where(lane == k, ix, idx_out)
        val_out = jnp.where(lane == k, e, val_out)
        logits = jnp.where(lane == ix, -jnp.inf, logits)
    idx_ref[...] = idx_out
    gate_ref[...] = val_out / den


def _mix_norm_route(x, os_, ws, g, b, wr, br, tm=ROW_TILE):
    n_tok = x.shape[0]
    n = len(os_)
    return pl.pallas_call(
        functools.partial(_mix_norm_kernel, n_mix=n),
        out_shape=[jax.ShapeDtypeStruct((n_tok, D_MODEL), F32),
                   jax.ShapeDtypeStruct((n_tok, LANES), I32),
                   jax.ShapeDtypeStruct((n_tok, LANES), F32)],
        grid=(n_tok // tm,),
        in_specs=[pl.BlockSpec((tm, D_MODEL), lambda i: (i, 0)),
                  pl.BlockSpec((1, D_MODEL), lambda i: (0, 0)),
                  pl.BlockSpec((1, D_MODEL), lambda i: (0, 0)),
                  pl.BlockSpec((D_MODEL, LANES), lambda i: (0, 0)),
                  pl.BlockSpec((1, LANES), lambda i: (0, 0))]
        + [pl.BlockSpec((tm, o.shape[1]), lambda i: (i, 0)) for o in os_]
        + [pl.BlockSpec(w.shape, lambda i: (0, 0)) for w in ws],
        out_specs=[pl.BlockSpec((tm, D_MODEL), lambda i: (i, 0)),
                   pl.BlockSpec((tm, LANES), lambda i: (i, 0)),
                   pl.BlockSpec((tm, LANES), lambda i: (i, 0))],
        compiler_params=_cparams(1),
        name="mix_norm_route",
    )(x, g, b, wr, br, *os_, *ws)


def _moe_kernel(be_ref, nb_ref, x_ref, gate_ref, wgu_ref, bgu_ref, wd_ref, bd_ref, y_ref, wgu_sc, wd_sc):
    i = pl.program_id(0)
    used = i < nb_ref[0]
    new_expert = (i == 0) | (be_ref[i] != be_ref[jnp.maximum(i - 1, 0)])

    @pl.when(used & new_expert)
    def _():
        for r in range(0, D_MODEL, CAST_ROWS):
            wgu_sc[r:r + CAST_ROWS, :] = wgu_ref[0, r:r + CAST_ROWS, :].astype(BF16)
        for r in range(0, D_FF, CAST_ROWS):
            wd_sc[r:r + CAST_ROWS, :] = wd_ref[0, r:r + CAST_ROWS, :].astype(BF16)

    @pl.when(used)
    def _():
        h = _dot(x_ref[...].astype(BF16), wgu_sc[...]) + bgu_ref[0]
        glu = jnp.minimum(h[:, :D_FF], SWIGLU_LIMIT)
        lin = jnp.clip(h[:, D_FF:], -SWIGLU_LIMIT, SWIGLU_LIMIT)
        act = glu * jax.nn.sigmoid(SWIGLU_ALPHA * glu) * (lin + 1.0)
        y = _dot(act.astype(BF16), wd_sc[...]) + bd_ref[0]
        y_ref[...] = y * gate_ref[...]

    @pl.when(i >= nb_ref[0])
    def _():
        y_ref[...] = jnp.zeros(y_ref.shape, y_ref.dtype)


def _moe_blocks(x_sorted, slot_gate, block_expert, n_used, wgu, bgu, wd, bd):
    n_slots = x_sorted.shape[0]
    n_blocks = n_slots // MOE_BLOCK
    grid_spec = pltpu.PrefetchScalarGridSpec(
        num_scalar_prefetch=2,
        grid=(n_blocks,),
        in_specs=[
            pl.BlockSpec((MOE_BLOCK, D_MODEL), lambda i, be, nb: (i, 0)),
            pl.BlockSpec((MOE_BLOCK, 1), lambda i, be, nb: (i, 0)),
            pl.BlockSpec((1, D_MODEL, 2 * D_FF), lambda i, be, nb: (be[i], 0, 0)),
            pl.BlockSpec((1, 1, 2 * D_FF), lambda i, be, nb: (be[i], 0, 0)),
            pl.BlockSpec((1, D_FF, D_MODEL), lambda i, be, nb: (be[i], 0, 0)),
            pl.BlockSpec((1, 1, D_MODEL), lambda i, be, nb: (be[i], 0, 0)),
        ],
        out_specs=pl.BlockSpec((MOE_BLOCK, D_MODEL), lambda i, be, nb: (i, 0)),
        scratch_shapes=[pltpu.VMEM((D_MODEL, 2 * D_FF), BF16), pltpu.VMEM((D_FF, D_MODEL), BF16)],
    )
    return pl.pallas_call(
        _moe_kernel,
        out_shape=jax.ShapeDtypeStruct((n_slots, D_MODEL), F32),
        grid_spec=grid_spec,
        compiler_params=_cparams(1),
        name="moe_blocks",
    )(block_expert, n_used, x_sorted, slot_gate, wgu, bgu, wd, bd)


def _ffn_norm_kernel(x_ref, yg_ref, g_ref, b_ref, o_ref):
    ffn = yg_ref[0] + yg_ref[1] + yg_ref[2] + yg_ref[3]
    o_ref[...] = _layer_norm(DEEPNORM_ALPHA * x_ref[...] + ffn, g_ref[...], b_ref[...])


def _ffn_norm(x, yg, g, b, tm=ROW_TILE):
    n_tok = x.shape[0]
    return pl.pallas_call(
        _ffn_norm_kernel,
        out_shape=jax.ShapeDtypeStruct((n_tok, D_MODEL), F32),
        grid=(n_tok // tm,),
        in_specs=[pl.BlockSpec((tm, D_MODEL), lambda i: (i, 0)),
                  pl.BlockSpec((TOP_K, tm, D_MODEL), lambda i: (0, i, 0)),
                  pl.BlockSpec((1, D_MODEL), lambda i: (0, 0)),
                  pl.BlockSpec((1, D_MODEL), lambda i: (0, 0))],
        out_specs=pl.BlockSpec((tm, D_MODEL), lambda i: (i, 0)),
        compiler_params=_cparams(1),
        name="ffn_norm",
    )(x, yg, g, b)


def _moe_layer(x, top_idx, gate, expert0, wgu, bgu, wd, bd, g, b):
    n_tok = x.shape[0]
    n_assign = n_tok * TOP_K
    e_flat = top_idx.reshape(n_assign)
    onehot = (e_flat[:, None] == jnp.arange(N_EXPERTS, dtype=I32)[None, :]).astype(I32)
    csum = jnp.cumsum(onehot, axis=0)
    counts = csum[-1]
    pos = jnp.take_along_axis(csum, e_flat[:, None], axis=1)[:, 0] - 1
    padded = (counts + MOE_BLOCK - 1) // MOE_BLOCK * MOE_BLOCK
    pad_end = jnp.cumsum(padded)
    pad_start = pad_end - padded
    slot = pad_start[e_flat] + pos
    n_blocks = -(-n_assign // MOE_BLOCK) + N_EXPERTS
    n_slots = n_blocks * MOE_BLOCK
    n_used = (pad_end[-1] // MOE_BLOCK).astype(I32)
    blk_start = jnp.arange(n_blocks, dtype=I32) * MOE_BLOCK
    block_expert = jnp.minimum(jnp.sum((pad_end[None, :] <= blk_start[:, None]).astype(I32), axis=1), N_EXPERTS - 1)
    order = jnp.argsort(e_flat, stable=True).astype(I32)
    first = jnp.cumsum(counts) - counts
    r_in = (blk_start - pad_start[block_expert])[:, None] + jnp.arange(MOE_BLOCK, dtype=I32)[None, :]
    live = (r_in < counts[block_expert][:, None]) & (blk_start < pad_end[-1])[:, None]
    src = jnp.clip(first[block_expert][:, None] + r_in, 0, n_assign - 1)
    slot_assign = jnp.where(live, order[src], n_assign).reshape(n_slots)
    slot_tok = jnp.minimum(slot_assign // TOP_K, n_tok)
    slot_gate = jnp.concatenate([gate.reshape(n_assign), jnp.zeros((1,), F32)])[slot_assign]
    last_expert = block_expert[jnp.maximum(n_used - 1, 0)]
    block_expert = jnp.where(jnp.arange(n_blocks) < n_used, block_expert, last_expert)
    x_pad = jnp.concatenate([x, jnp.zeros((1, D_MODEL), F32)], axis=0)
    x_sorted = x_pad[slot_tok]
    y_slots = _moe_blocks(x_sorted, slot_gate[:, None], block_expert + expert0, n_used[None], wgu, bgu, wd, bd)
    yg = y_slots[slot.reshape(n_tok, TOP_K).T]
    return _ffn_norm(x, yg, g, b)


def _alibi_slopes(n_heads):
    return jnp.exp2(-8.0 * jnp.arange(1, n_heads + 1, dtype=F32) / n_heads)


def _block_diag_rows(q, n_heads, width):
    n_seq = q.shape[0] // DEC_SEQ
    q4 = q.reshape(n_seq, DEC_SEQ, n_heads, width)
    eye = jnp.eye(n_heads, dtype=q.dtype)
    out = jnp.einsum('sqhd,hg->shqgd', q4, eye)
    return out.reshape(n_seq, n_heads * DEC_SEQ, n_heads * width)


def kernel(x_prompt, x_sample, cache_kv_even, cache_kv_odd, cache_kidx_odd, page_table,
           w_in_even, w_o_even, w_in_odd, w_o_odd, ln_g, ln_b,
           w_router, b_router, w_gate_up, b_gate_up, w_down, b_down):
    batch, seq, _ = x_prompt.shape
    n_seq, dec_seq, _ = x_sample.shape
    n_p = batch * seq
    n_s = n_seq * dec_seq
    hd = N_HEADS * HEAD_DIM
    x = jnp.concatenate([x_prompt.reshape(n_p, D_MODEL), x_sample.reshape(n_s, D_MODEL)], axis=0)

    slopes_even = _alibi_slopes(H_MOBA)
    slopes_odd = _alibi_slopes(N_HEADS)
    wr = jnp.pad(w_router, ((0, 0), (0, 0), (0, LANES - N_EXPERTS))).astype(BF16)
    br = jnp.pad(b_router, ((0, 0), (0, LANES - N_EXPERTS)), constant_values=-jnp.inf)[:, None, :]
    ln_g4 = ln_g[:, :, None, :]
    ln_b4 = ln_b[:, :, None, :]

    def moe(layer, x_mid, top_idx, gate):
        return _moe_layer(x_mid, top_idx[:, :TOP_K], gate[:, :TOP_K], layer * N_EXPERTS,
                          w_gate_up.reshape(-1, D_MODEL, 2 * D_FF), b_gate_up.reshape(-1, 1, 2 * D_FF),
                          w_down.reshape(-1, D_FF, D_MODEL), b_down.reshape(-1, 1, D_MODEL),
                          ln_g4[layer, 1], ln_b4[layer, 1])

    w_in = w_in_even[0].astype(BF16)
    ws = [w_in[:, :hd], w_in[:, hd:]]
    q_p, kv_p, kv_t = _proj(x, 0, n_p, ws, [w_in[:, hd:].T], seq, tm=PROMPT_PROJ_TILE)
    q_s, kv_s = _proj(x, n_p, n_s, ws)
    o_moba = _moba_prompt(q_p, kv_p, slopes_even, batch, seq)
    o_sb = _sb_prompt(q_p, kv_p, batch, seq)
    qrows = _block_diag_rows(q_s.astype(BF16), N_HEADS, HEAD_DIM)
    o_s = _even_decode(qrows, kv_s, _pool_pages_kv(cache_kv_even), page_table, slopes_even)
    o_all = jnp.concatenate([jnp.concatenate([o_moba, o_sb], axis=1), o_s], axis=0)
    w_o = w_o_even[0].astype(BF16)
    x, top_idx, gate = _mix_norm_route(x, [o_all], [w_o], ln_g4[0, 0], ln_b4[0, 0], wr[0], br[0])
    x = moe(0, x, top_idx, gate)
    new_kv_even_p = _kv_from_feature_major(kv_t, seq)
    new_kv_even_s = kv_s.reshape(1, n_seq, dec_seq, 2, N_HEADS, HEAD_DIM)

    w_in = w_in_odd[0].astype(BF16)
    n_qkv = 3 * hd
    n_qi = IDX_HEADS * IDX_DIM
    w_kidx = w_in[:, n_qkv + n_qi:n_qkv + n_qi + IDX_DIM]
    w_wi = jnp.pad(w_in[:, n_qkv + n_qi + IDX_DIM:], ((0, 0), (0, LANES - IDX_HEADS)))
    ws = [w_in[:, :hd], w_in[:, hd:n_qkv], w_in[:, n_qkv:n_qkv + n_qi],
          jnp.concatenate([w_kidx, w_kidx], axis=1), w_wi]
    q_p, kv_p, qi_p, kidx2_p, wi_p, kv_t = _proj(x, 0, n_p, ws, [w_in[:, hd:n_qkv].T], seq, tm=PROMPT_PROJ_TILE)
    q_s, kv_s, qi_s, kidx2_s, wi_s = _proj(x, n_p, n_s, ws)
    mask = _dsa_select_prompt(qi_p, wi_p, kidx2_p, batch, seq)
    o_p = _dsa_attn_prompt(q_p, kv_p, mask, slopes_odd, batch, seq)
    qrows = _block_diag_rows(q_s.astype(BF16), N_HEADS, HEAD_DIM)
    qirows = qi_s.astype(BF16).reshape(n_seq, DEC_SEQ, IDX_HEADS, IDX_DIM).transpose(0, 2, 1, 3)
    qirows = qirows.reshape(n_seq, IDX_HEADS * DEC_SEQ, IDX_DIM)
    kidx_s = kidx2_s[:, :IDX_DIM]
    pool_idx = jnp.transpose(cache_kidx_odd[0], (0, 2, 1))
    o_s = _odd_decode(qrows, qirows, wi_s, kv_s, kidx_s, _pool_pages_kv(cache_kv_odd), pool_idx,
                      page_table, slopes_odd)
    o_all = jnp.concatenate([o_p, o_s], axis=0)
    w_o = w_o_odd[0].astype(BF16)
    x, top_idx, gate = _mix_norm_route(x, [o_all], [w_o], ln_g4[1, 0], ln_b4[1, 0], wr[1], br[1])
    x = moe(1, x, top_idx, gate)
    new_kv_odd_p = _kv_from_feature_major(kv_t, seq)
    new_kv_odd_s = kv_s.reshape(1, n_seq, dec_seq, 2, N_HEADS, HEAD_DIM)
    new_kidx_p = kidx2_p[:, :IDX_DIM].reshape(1, batch, seq, IDX_DIM)
    new_kidx_s = kidx_s.reshape(1, n_seq, dec_seq, IDX_DIM)

    y_prompt = x[:n_p].reshape(batch, seq, D_MODEL)
    y_sample = x[n_p:].reshape(n_seq, dec_seq, D_MODEL)
    return (y_prompt, y_sample, new_kv_even_p, new_kv_even_s, new_kv_odd_p, new_kv_odd_s,
            new_kidx_p, new_kidx_s)
```
